```python
import math
import jax, jax.numpy as jnp
from jax import lax
import numpy as np

D_MODEL = 2048
BATCH = 4
SEQ = 2048
DEPTH = 4
DEC_BATCH = 8
DEC_SEQ = 4
PAST_LEN = 16384
PAGE_SIZE = 128

W_ATTN = D_MODEL // 2
ATTN_HEAD_DIM = 128
ATTN_HEADS = W_ATTN // ATTN_HEAD_DIM
W_SSM = D_MODEL // 4
SSM_CH = 16
SSM_GROUPS = W_SSM // SSM_CH
SSM_STATE = 64
W_POOL = D_MODEL // 4
POOL_WINDOWS = (2, 4, 8, 16)
POOL_GROUPS = len(POOL_WINDOWS)
POOL_CH = W_POOL // POOL_GROUPS
POOL_BUF = max(POOL_WINDOWS) - 1
W_IN = 3 * W_ATTN + W_SSM + W_POOL
MIX_W = W_ATTN + W_SSM + W_POOL
D_FF = ((8 * D_MODEL // 3 + 255) // 256) * 256
Q_BLOCK = 128
RMS_EPS = 1e-6
DT_MIN = 1e-3
DT_MAX = 1e-1
SB_LOGIT_SHIFT = 8.0
SB_BIAS_NORM = math.sqrt(SB_LOGIT_SHIFT * math.sqrt(ATTN_HEAD_DIM))

kernel_name = "hymba_s5_pool_stickbreak_macaron"

F32 = jnp.float32


def rmsnorm(x, g):
    xf = x.astype(F32)
    y = xf * lax.rsqrt(jnp.mean(xf * xf, axis=-1, keepdims=True) + RMS_EPS)
    return (y * g.astype(F32)).astype(x.dtype)


def swiglu(h, w_gate, w_up, w_down):
    return (jax.nn.silu(h @ w_gate) * (h @ w_up)) @ w_down


def stick_breaking_attention(q, k, v, pos0):
    bsz, t_len, n_h, d_h = q.shape
    s_len = k.shape[1]
    blk = min(Q_BLOCK, t_len)
    n_blk = -(-t_len // blk)
    t_pad = n_blk * blk
    qf = jnp.pad(q.astype(F32), ((0, 0), (0, t_pad - t_len), (0, 0), (0, 0)))
    q_blocks = qf.reshape(bsz, n_blk, blk, n_h, d_h).transpose(1, 0, 2, 3, 4)
    q_pos = (pos0 + jnp.arange(t_pad, dtype=jnp.int32)).reshape(n_blk, blk)
    k_pos = jnp.arange(s_len, dtype=jnp.int32)
    kf = k.astype(F32)
    vf = v.astype(F32)
    scale = 1.0 / math.sqrt(d_h)

    def one_block(args):
        qb, qp = args
        z = jnp.einsum("bqhd,bkhd->bhqk", qb, kf) * scale
        mask = k_pos[None, :] < qp[:, None]
        log_keep = jnp.where(mask, jax.nn.log_sigmoid(-z), 0.0)
        log_skip = lax.cumsum(log_keep, axis=3, reverse=True) - log_keep
        w = jnp.where(mask, jnp.exp(jax.nn.log_sigmoid(z) + log_skip), 0.0)
        return jnp.einsum("bhqk,bkhd->bqhd", w, vf)

    out = lax.map(one_block, (q_blocks, q_pos))
    out = out.transpose(1, 0, 2, 3, 4).reshape(bsz, t_pad, n_h * d_h)
    return out[:, :t_len]


def _complex_scan_op(e1, e2):
    ar1, ai1, br1, bi1 = e1
    ar2, ai2, br2, bi2 = e2
    return (ar2 * ar1 - ai2 * ai1,
            ar2 * ai1 + ai2 * ar1,
            ar2 * br1 - ai2 * bi1 + br2,
            ar2 * bi1 + ai2 * br1 + bi2)


def s5_mix(u, h0_re, h0_im, lam_re, lam_im, log_dt, b_re, b_im, c_re, c_im, d_skip, w_glu):
    bsz, t_len, _ = u.shape
    uf = u.astype(F32).reshape(bsz, t_len, SSM_GROUPS, SSM_CH)
    lr = lam_re.astype(F32)
    li = lam_im.astype(F32)
    dt = jnp.exp(log_dt.astype(F32))[:, None]
    mag = jnp.exp(lr * dt)
    ang = li * dt
    lb_re = mag * jnp.cos(ang)
    lb_im = mag * jnp.sin(ang)
    nr = lb_re - 1.0
    ni = lb_im
    den = lr * lr + li * li
    fr = (nr * lr + ni * li) / den
    fi = (ni * lr - nr * li) / den
    br = b_re.astype(F32)
    bi = b_im.astype(F32)
    bb_re = fr[..., None] * br - fi[..., None] * bi
    bb_im = fr[..., None] * bi + fi[..., None] * br
    bu_re = jnp.einsum("btgh,gph->btgp", uf, bb_re)
    bu_im = jnp.einsum("btgh,gph->btgp", uf, bb_im)
    a_re = jnp.broadcast_to(lb_re, bu_re.shape)
    a_im = jnp.broadcast_to(lb_im, bu_im.shape)
    cum_re, cum_im, s_re, s_im = lax.associative_scan(
        _complex_scan_op, (a_re, a_im, bu_re, bu_im), axis=1)
    h0r = h0_re.astype(F32)[:, None]
    h0i = h0_im.astype(F32)[:, None]
    h_re = cum_re * h0r - cum_im * h0i + s_re
    h_im = cum_re * h0i + cum_im * h0r + s_im
    y = (jnp.einsum("btgp,ghp->btgh", h_re, c_re.astype(F32))
         - jnp.einsum("btgp,ghp->btgh", h_im, c_im.astype(F32))
         + d_skip.astype(F32).reshape(SSM_GROUPS, SSM_CH) * uf)
    y = jax.nn.gelu(y.reshape(bsz, t_len, W_SSM))
    y = y * jax.nn.sigmoid(y @ w_glu.astype(F32))
    return y.astype(u.dtype), h_re[:, -1], h_im[:, -1]


def pool_mix(u, buf, pos0, w_pool, pool_scale):
    bsz, t_len, _ = u.shape
    ext_raw = jnp.concatenate([buf.astype(u.dtype), u], axis=1)
    ext = ext_raw.astype(F32)
    cs = jnp.concatenate([jnp.zeros((bsz, 1, W_POOL), F32), jnp.cumsum(ext, axis=1)], axis=1)
    hi = cs[:, POOL_BUF + 1:]
    pos = pos0 + jnp.arange(t_len, dtype=jnp.int32)
    uf = u.astype(F32)
    diffs = []
    for g, win in enumerate(POOL_WINDOWS):
        sl = slice(g * POOL_CH, (g + 1) * POOL_CH)
        lo = cs[:, POOL_BUF + 1 - win: POOL_BUF + 1 - win + t_len, sl]
        cnt = jnp.minimum(pos + 1, win).astype(F32)[None, :, None]
        diffs.append((hi[..., sl] - lo) / cnt - uf[..., sl])
    diff = jnp.stack(diffs, axis=2)
    out = jnp.einsum("btgc,gcd->btgd", diff, w_pool.astype(F32)).reshape(bsz, t_len, W_POOL)
    out = out * pool_scale.astype(F32)
    return out.astype(u.dtype), ext_raw[:, -POOL_BUF:]


def decoder_layer(x, past_k, past_v, h0_re, h0_im, pool_buf, p):
    bsz, t_len, _ = x.shape
    pos0 = past_k.shape[1]
    h = rmsnorm(x, p["norm_ffn1"])
    x = x + 0.5 * swiglu(h, p["ffn1_w_gate"], p["ffn1_w_up"], p["ffn1_w_down"])
    h = rmsnorm(x, p["norm_mix"])
    proj = h @ p["w_in"] + p["b_in"]
    q, k, v, u_ssm, u_pool = jnp.split(
        proj, [W_ATTN, 2 * W_ATTN, 3 * W_ATTN, 3 * W_ATTN + W_SSM], axis=-1)
    q = q.reshape(bsz, t_len, ATTN_HEADS, ATTN_HEAD_DIM)
    k = k.reshape(bsz, t_len, ATTN_HEADS, ATTN_HEAD_DIM)
    v = v.reshape(bsz, t_len, ATTN_HEADS, ATTN_HEAD_DIM)
    k_all = jnp.concatenate([past_k.astype(k.dtype), k], axis=1)
    v_all = jnp.concatenate([past_v.astype(v.dtype), v], axis=1)
    att = stick_breaking_attention(q, k_all, v_all, pos0).astype(x.dtype)
    ssm, h_re, h_im = s5_mix(u_ssm, h0_re, h0_im, p["ssm_lambda_re"], p["ssm_lambda_im"],
                             p["ssm_log_dt"], p["ssm_b_re"], p["ssm_b_im"], p["ssm_c_re"],
                             p["ssm_c_im"], p["ssm_d"], p["ssm_w_glu"])
    pool, new_buf = pool_mix(u_pool, pool_buf, pos0, p["pool_w"], p["pool_scale"])
    merged = jnp.concatenate([rmsnorm(att, p["norm_attn_out"]),
                              rmsnorm(ssm, p["norm_ssm_out"]),
                              rmsnorm(pool, p["norm_pool_out"])], axis=-1)
    x = x + merged @ p["w_out"]
    h = rmsnorm(x, p["norm_ffn2"])
    x = x + 0.5 * swiglu(h, p["ffn2_w_gate"], p["ffn2_w_up"], p["ffn2_w_down"])
    return x, k, v, h_re, h_im, new_buf


def setup_inputs(seed: int = 0) -> dict:
    key = jax.random.key(seed)
    ks = iter(jax.random.split(key, 64))

    def nrm(shape, scale):
        return jax.random.normal(next(ks), shape, F32) * scale

    def gain(shape):
        return 1.0 + nrm(shape, 0.02)

    n_pages = PAST_LEN // PAGE_SIZE
    n_pool = (DEC_BATCH * n_pages * 5) // 4
    page_table = jax.random.permutation(next(ks), n_pool)[: DEC_BATCH * n_pages]
    page_table = page_table.reshape(DEC_BATCH, n_pages).astype(jnp.int32)
    n_idx = jnp.arange(SSM_STATE, dtype=F32)
    gp = (DEPTH, SSM_GROUPS, SSM_STATE)
    qk_dir = nrm((DEPTH, ATTN_HEADS, ATTN_HEAD_DIM), 1.0)
    qk_dir = qk_dir / jnp.sqrt(jnp.sum(qk_dir * qk_dir, axis=-1, keepdims=True))
    b_q = SB_BIAS_NORM * qk_dir
    b_k = -SB_BIAS_NORM * qk_dir
    b_in = nrm((DEPTH, W_IN), 0.02)
    b_in = b_in.at[:, :W_ATTN].add(b_q.reshape(DEPTH, W_ATTN))
    b_in = b_in.at[:, W_ATTN:2 * W_ATTN].add(b_k.reshape(DEPTH, W_ATTN))
    kv_shape = (DEPTH, n_pool, PAGE_SIZE, ATTN_HEADS, ATTN_HEAD_DIM)
    cache_k = nrm(kv_shape, 1.0) + b_k[:, None, None]
    cache_v = nrm(kv_shape, 1.0)
    return {
        "x_prompt": nrm((BATCH, SEQ, D_MODEL), 1.0),
        "x_sample": nrm((DEC_BATCH, DEC_SEQ, D_MODEL), 1.0),
        "cache_k": cache_k,
        "cache_v": cache_v,
        "state_ssm_re": nrm((DEPTH, DEC_BATCH, SSM_GROUPS, SSM_STATE), 0.5),
        "state_ssm_im": nrm((DEPTH, DEC_BATCH, SSM_GROUPS, SSM_STATE), 0.5),
        "state_pool": nrm((DEPTH, DEC_BATCH, POOL_BUF, W_POOL), 1.0),
        "page_table": page_table,
        "norm_ffn1": gain((DEPTH, D_MODEL)),
        "ffn1_w_gate": nrm((DEPTH, D_MODEL, D_FF), D_MODEL ** -0.5),
        "ffn1_w_up": nrm((DEPTH, D_MODEL, D_FF), D_MODEL ** -0.5),
        "ffn1_w_down": nrm((DEPTH, D_FF, D_MODEL), D_FF ** -0.5),
        "norm_mix": gain((DEPTH, D_MODEL)),
        "w_in": nrm((DEPTH, D_MODEL, W_IN), D_MODEL ** -0.5),
        "b_in": b_in,
        "ssm_lambda_re": -0.5 + nrm(gp, 0.01),
        "ssm_lambda_im": math.pi * n_idx + nrm(gp, 0.01),
        "ssm_log_dt": jax.random.uniform(next(ks), (DEPTH, SSM_GROUPS), F32,
                                         math.log(DT_MIN), math.log(DT_MAX)),
        "ssm_b_re": nrm((DEPTH, SSM_GROUPS, SSM_STATE, SSM_CH), (2 * SSM_CH) ** -0.5),
        "ssm_b_im": nrm((DEPTH, SSM_GROUPS, SSM_STATE, SSM_CH), (2 * SSM_CH) ** -0.5),
        "ssm_c_re": nrm((DEPTH, SSM_GROUPS, SSM_CH, SSM_STATE), (2 * SSM_STATE) ** -0.5),
        "ssm_c_im": nrm((DEPTH, SSM_GROUPS, SSM_CH, SSM_STATE), (2 * SSM_STATE) ** -0.5),
        "ssm_d": nrm((DEPTH, W_SSM), 1.0),
        "ssm_w_glu": nrm((DEPTH, W_SSM, W_SSM), W_SSM ** -0.5),
        "pool_w": nrm((DEPTH, POOL_GROUPS, POOL_CH, POOL_CH), POOL_CH ** -0.5),
        "pool_scale": gain((DEPTH, W_POOL)),
        "norm_attn_out": gain((DEPTH, W_ATTN)),
        "norm_ssm_out": gain((DEPTH, W_SSM)),
        "norm_pool_out": gain((DEPTH, W_POOL)),
        "w_out": nrm((DEPTH, MIX_W, D_MODEL), MIX_W ** -0.5),
        "norm_ffn2": gain((DEPTH, D_MODEL)),
        "ffn2_w_gate": nrm((DEPTH, D_MODEL, D_FF), D_MODEL ** -0.5),
        "ffn2_w_up": nrm((DEPTH, D_MODEL, D_FF), D_MODEL ** -0.5),
        "ffn2_w_down": nrm((DEPTH, D_FF, D_MODEL), D_FF ** -0.5),
        "norm_final": gain((D_MODEL,)),
    }


def reference(x_prompt, x_sample, cache_k, cache_v, state_ssm_re, state_ssm_im, state_pool,
              page_table, norm_ffn1, ffn1_w_gate, ffn1_w_up, ffn1_w_down, norm_mix, w_in, b_in,
              ssm_lambda_re, ssm_lambda_im, ssm_log_dt, ssm_b_re, ssm_b_im, ssm_c_re, ssm_c_im,
              ssm_d, ssm_w_glu, pool_w, pool_scale, norm_attn_out, norm_ssm_out, norm_pool_out,
              w_out, norm_ffn2, ffn2_w_gate, ffn2_w_up, ffn2_w_down, norm_final):
    b_p = x_prompt.shape[0]
    b_s = x_sample.shape[0]
    past_len = page_table.shape[1] * cache_k.shape[2]
    empty_kv = jnp.zeros((b_p, 0, ATTN_HEADS, ATTN_HEAD_DIM), x_prompt.dtype)
    zero_h = jnp.zeros((b_p, SSM_GROUPS, SSM_STATE), F32)
    zero_buf = jnp.zeros((b_p, POOL_BUF, W_POOL), x_prompt.dtype)
    xp, xs = x_prompt, x_sample
    kp_l, vp_l, hrp_l, hip_l, bp_l = [], [], [], [], []
    ks_l, vs_l, hrs_l, his_l, bs_l = [], [], [], [], []
    for l in range(DEPTH):
        p = dict(norm_ffn1=norm_ffn1[l], ffn1_w_gate=ffn1_w_gate[l], ffn1_w_up=ffn1_w_up[l],
                 ffn1_w_down=ffn1_w_down[l], norm_mix=norm_mix[l], w_in=w_in[l], b_in=b_in[l],
                 ssm_lambda_re=ssm_lambda_re[l], ssm_lambda_im=ssm_lambda_im[l],
                 ssm_log_dt=ssm_log_dt[l], ssm_b_re=ssm_b_re[l], ssm_b_im=ssm_b_im[l],
                 ssm_c_re=ssm_c_re[l], ssm_c_im=ssm_c_im[l], ssm_d=ssm_d[l],
                 ssm_w_glu=ssm_w_glu[l], pool_w=pool_w[l], pool_scale=pool_scale[l],
                 norm_attn_out=norm_attn_out[l], norm_ssm_out=norm_ssm_out[l],
                 norm_pool_out=norm_pool_out[l], w_out=w_out[l], norm_ffn2=norm_ffn2[l],
                 ffn2_w_gate=ffn2_w_gate[l], ffn2_w_up=ffn2_w_up[l], ffn2_w_down=ffn2_w_down[l])
        xp, k_new, v_new, hr, hi, buf = decoder_layer(xp, empty_kv, empty_kv, zero_h, zero_h,
                                                      zero_buf, p)
        kp_l.append(k_new); vp_l.append(v_new); hrp_l.append(hr); hip_l.append(hi); bp_l.append(buf)
        past_k = cache_k[l][page_table].reshape(b_s, past_len, ATTN_HEADS, ATTN_HEAD_DIM)
        past_v = cache_v[l][page_table].reshape(b_s, past_len, ATTN_HEADS, ATTN_HEAD_DIM)
        xs, k_new, v_new, hr, hi, buf = decoder_layer(xs, past_k, past_v, state_ssm_re[l],
                                                      state_ssm_im[l], state_pool[l], p)
        ks_l.append(k_new); vs_l.append(v_new); hrs_l.append(hr); his_l.append(hi); bs_l.append(buf)
    y_prompt = rmsnorm(xp, norm_final)
    y_sample = rmsnorm(xs, norm_final)
    return (y_prompt, y_sample,
            jnp.stack(kp_l), jnp.stack(vp_l), jnp.stack(hrp_l), jnp.stack(hip_l), jnp.stack(bp_l),
            jnp.stack(ks_l), jnp.stack(vs_l), jnp.stack(hrs_l), jnp.stack(his_l), jnp.stack(bs_l))
```

```python
import functools
import math

import jax
import jax.numpy as jnp
from jax import lax
from jax.experimental import pallas as pl
from jax.experimental.pallas import tpu as pltpu

F32 = jnp.float32
BF16 = jnp.bfloat16
RMS_EPS = 1e-6
LANES = 128
SUBLANES = 8
VMEM_LIMIT = 56 * 1024 * 1024

POOL_WINDOWS = (2, 4, 8, 16)
HALO = 16


def _params(sem, vmem=VMEM_LIMIT):
    return pltpu.CompilerParams(dimension_semantics=sem, vmem_limit_bytes=vmem)


def _rms_scale(x):
    return x * lax.rsqrt(jnp.mean(x * x, axis=-1, keepdims=True) + RMS_EPS)


def _dot(a, b):
    return jnp.dot(a, b, preferred_element_type=F32)


def _ffn_kernel(x_ref, g_ref, wg_ref, wu_ref, wd_ref, gf_ref, o_ref, hn_ref, *, final_norm):
    f = pl.program_id(1)

    @pl.when(f == 0)
    def _():
        hn_ref[...] = (_rms_scale(x_ref[...]) * g_ref[...]).astype(BF16)
        o_ref[...] = jnp.zeros_like(o_ref)

    hn = hn_ref[...]
    gate = _dot(hn, wg_ref[...].astype(BF16))
    up = _dot(hn, wu_ref[...].astype(BF16))
    act = (gate * jax.nn.sigmoid(gate) * up).astype(BF16)
    o_ref[...] += _dot(act, wd_ref[...].astype(BF16))

    @pl.when(f == pl.num_programs(1) - 1)
    def _():
        y = x_ref[...] + 0.5 * o_ref[...]
        if final_norm:
            y = _rms_scale(y) * gf_ref[...]
        o_ref[...] = y


def _ffn(x, g, wg, wu, wd, g_final, *, final_norm, tm, tf):
    m, d = x.shape
    d_ff = wg.shape[1]
    assert m % tm == 0 and d_ff % tf == 0
    return pl.pallas_call(
        functools.partial(_ffn_kernel, final_norm=final_norm),
        grid=(m // tm, d_ff // tf),
        in_specs=[
            pl.BlockSpec((tm, d), lambda i, f: (i, 0), pipeline_mode=pl.Buffered(1)),
            pl.BlockSpec((1, d), lambda i, f: (0, 0)),
            pl.BlockSpec((d, tf), lambda i, f: (0, f)),
            pl.BlockSpec((d, tf), lambda i, f: (0, f)),
            pl.BlockSpec((tf, d), lambda i, f: (f, 0)),
            pl.BlockSpec((1, d), lambda i, f: (0, 0)),
        ],
        out_specs=pl.BlockSpec((tm, d), lambda i, f: (i, 0)),
        out_shape=jax.ShapeDtypeStruct((m, d), F32),
        scratch_shapes=[pltpu.VMEM((tm, d), BF16)],
        compiler_params=_params(("parallel", "arbitrary")),
        name="ffn",
    )(x, g.reshape(1, d), wg, wu, wd, g_final.reshape(1, d))


def _in_proj_kernel(x_ref, g_ref, w_ref, b_ref, q_ref, k_ref, v_ref, us_ref, up_ref, hn_ref, *, na):
    n = pl.program_id(1)

    @pl.when(n == 0)
    def _():
        hn_ref[...] = (_rms_scale(x_ref[...]) * g_ref[...]).astype(BF16)

    y = _dot(hn_ref[...], w_ref[...].astype(BF16)) + b_ref[...]

    @pl.when(n < na)
    def _():
        q_ref[...] = y.astype(BF16)

    @pl.when((n >= na) & (n < 2 * na))
    def _():
        k_ref[...] = y

    @pl.when((n >= 2 * na) & (n < 3 * na))
    def _():
        v_ref[...] = y

    @pl.when(n == 3 * na)
    def _():
        us_ref[...] = y

    @pl.when(n == 3 * na + 1)
    def _():
        up_ref[...] = y


def _in_proj(x, g, w, b, *, w_attn, w_ssm, w_pool, tm):
    m, d = x.shape
    tn = w_ssm
    assert w_pool == tn and w_attn % tn == 0 and m % tm == 0
    na = w_attn // tn
    n_steps = 3 * na + 2
    assert w.shape[1] == n_steps * tn

    def attn_map(first):
        return lambda i, n: (i, jnp.clip(n - first, 0, na - 1))

    return pl.pallas_call(
        functools.partial(_in_proj_kernel, na=na),
        grid=(m // tm, n_steps),
        in_specs=[
            pl.BlockSpec((tm, d), lambda i, n: (i, 0)),
            pl.BlockSpec((1, d), lambda i, n: (0, 0)),
            pl.BlockSpec((d, tn), lambda i, n: (0, n)),
            pl.BlockSpec((1, tn), lambda i, n: (0, n)),
        ],
        out_specs=[
            pl.BlockSpec((tm, tn), attn_map(0)),
            pl.BlockSpec((tm, tn), attn_map(na)),
            pl.BlockSpec((tm, tn), attn_map(2 * na)),
            pl.BlockSpec((tm, tn), lambda i, n: (i, 0)),
            pl.BlockSpec((tm, tn), lambda i, n: (i, 0)),
        ],
        out_shape=[
            jax.ShapeDtypeStruct((m, w_attn), BF16),
            jax.ShapeDtypeStruct((m, w_attn), F32),
            jax.ShapeDtypeStruct((m, w_attn), F32),
            jax.ShapeDtypeStruct((m, w_ssm), F32),
            jax.ShapeDtypeStruct((m, w_pool), F32),
        ],
        scratch_shapes=[pltpu.VMEM((tm, d), BF16)],
        compiler_params=_params(("parallel", "arbitrary")),
        name="in_proj",
    )(x, g.reshape(1, d), w, b.reshape(1, -1))


def _out_proj_kernel(x_ref, a_ref, s_ref, p_ref, ga_ref, gs_ref, gp_ref, w_ref, o_ref, m_ref,
                     *, w_attn, w_ssm):
    n = pl.program_id(1)

    @pl.when(n == 0)
    def _():
        m_ref[:, 0:w_attn] = (_rms_scale(a_ref[...]) * ga_ref[...]).astype(BF16)
        m_ref[:, w_attn:w_attn + w_ssm] = (_rms_scale(s_ref[...]) * gs_ref[...]).astype(BF16)
        m_ref[:, w_attn + w_ssm:] = (_rms_scale(p_ref[...]) * gp_ref[...]).astype(BF16)

    o_ref[...] = x_ref[...] + _dot(m_ref[...], w_ref[...].astype(BF16))


def _out_proj(x, att, ssm, pool, ga, gs, gp, w, *, tm, tn):
    m, d = x.shape
    wa, ws, wp = att.shape[1], ssm.shape[1], pool.shape[1]
    mix = wa + ws + wp
    assert w.shape == (mix, d) and m % tm == 0 and d % tn == 0
    return pl.pallas_call(
        functools.partial(_out_proj_kernel, w_attn=wa, w_ssm=ws),
        grid=(m // tm, d // tn),
        in_specs=[
            pl.BlockSpec((tm, tn), lambda i, n: (i, n)),
            pl.BlockSpec((tm, wa), lambda i, n: (i, 0)),
            pl.BlockSpec((tm, ws), lambda i, n: (i, 0)),
            pl.BlockSpec((tm, wp), lambda i, n: (i, 0)),
            pl.BlockSpec((1, wa), lambda i, n: (0, 0)),
            pl.BlockSpec((1, ws), lambda i, n: (0, 0)),
            pl.BlockSpec((1, wp), lambda i, n: (0, 0)),
            pl.BlockSpec((mix, tn), lambda i, n: (0, n)),
        ],
        out_specs=pl.BlockSpec((tm, tn), lambda i, n: (i, n)),
        out_shape=jax.ShapeDtypeStruct((m, d), F32),
        scratch_shapes=[pltpu.VMEM((tm, mix), BF16)],
        compiler_params=_params(("parallel", "arbitrary")),
        name="out_proj",
    )(x, att, ssm, pool, ga.reshape(1, wa), gs.reshape(1, ws), gp.reshape(1, wp), w)


def _sb_terms(z):
    ls = jnp.minimum(z, 0.0) - jnp.log1p(jnp.exp(-jnp.abs(z)))
    return ls, ls - z


def _split_bf16(x):
    hi = x.astype(BF16)
    lo = (x - hi.astype(F32)).astype(BF16)
    return hi, lo


def _attn_prompt_kernel(q_ref, k_ref, v_ref, o_ref, acc_ref, carry_ref, *, blk, scale):
    qi = pl.program_id(2)
    q = q_ref[...]
    row = lax.broadcasted_iota(jnp.int32, (blk, blk), 0)
    col = lax.broadcasted_iota(jnp.int32, (blk, blk), 1)
    after = jnp.where(row > col, 1.0, 0.0).astype(BF16)
    causal = col < row

    def block(start, mask):
        kb = k_ref[pl.ds(start, blk), :].astype(BF16)
        vb = v_ref[pl.ds(start, blk), :].astype(BF16)
        z = lax.dot_general(q, kb, (((1,), (1,)), ((), ())), preferred_element_type=F32) * scale
        ls, lk = _sb_terms(z)
        if mask is not None:
            lk = jnp.where(mask, lk, 0.0)
        hi, lo = _split_bf16(lk)
        skip = _dot(hi, after) + _dot(lo, after)
        w = jnp.exp(ls + skip + carry_ref[...])
        if mask is not None:
            w = jnp.where(mask, w, 0.0)
        acc_ref[...] += _dot(w.astype(BF16), vb)
        carry_ref[...] += skip[:, 0:1] + lk[:, 0:1]

    acc_ref[...] = jnp.zeros_like(acc_ref)
    carry_ref[...] = jnp.zeros_like(carry_ref)
    block(pl.multiple_of(qi * blk, blk), causal)

    def body(i, c):
        block(pl.multiple_of((qi - 1 - i) * blk, blk), None)
        return c

    lax.fori_loop(0, qi, body, 0)
    o_ref[...] = acc_ref[...]


def _attn_prompt(q, k, v, *, bsz, t_len, n_heads, d_head, blk):
    assert t_len % blk == 0 and d_head == LANES
    nq = t_len // blk
    return pl.pallas_call(
        functools.partial(_attn_prompt_kernel, blk=blk, scale=1.0 / math.sqrt(d_head)),
        grid=(bsz, n_heads, nq),
        in_specs=[
            pl.BlockSpec((blk, d_head), lambda b, h, i: (b * nq + i, h)),
            pl.BlockSpec((t_len, d_head), lambda b, h, i: (b, h)),
            pl.BlockSpec((t_len, d_head), lambda b, h, i: (b, h)),
        ],
        out_specs=pl.BlockSpec((blk, d_head), lambda b, h, i: (b * nq + i, h)),
        out_shape=jax.ShapeDtypeStruct(k.shape, F32),
        scratch_shapes=[pltpu.VMEM((blk, d_head), F32), pltpu.VMEM((blk, 1), F32)],
        compiler_params=_params(("parallel", "parallel", "arbitrary")),
        name="attn_prompt",
    )(q, k, v)


def _attn_sample_kernel(pt_ref, qbd_ref, kn_ref, vn_ref, *rest, pages_per_step, n_heads, n_tok, scale):
    del pt_ref
    k_refs = rest[:pages_per_step]
    v_refs = rest[pages_per_step:2 * pages_per_step]
    o_ref, acc_ref, carry_ref = rest[2 * pages_per_step:]
    j = pl.program_id(1)
    page = k_refs[0].shape[0]
    ncol = n_tok * n_heads
    qbd = qbd_ref[...]
    row = lax.broadcasted_iota(jnp.int32, (page, page), 0)
    col = lax.broadcasted_iota(jnp.int32, (page, page), 1)
    after = jnp.where(col > row, 1.0, 0.0).astype(BF16)

    def block(k_ref, v_ref, mask):
        z = _dot(k_ref[...].astype(BF16), qbd) * scale
        ls, lk = _sb_terms(z)
        if mask is not None:
            lk = jnp.where(mask, lk, 0.0)
        hi, lo = _split_bf16(lk)
        skip = _dot(after, hi) + _dot(after, lo)
        w = jnp.exp(ls + skip + carry_ref[...])
        if mask is not None:
            w = jnp.where(mask, w, 0.0)
        acc_ref[...] += lax.dot_general(w.astype(BF16), v_ref[...].astype(BF16),
                                        (((0,), (0,)), ((), ())), preferred_element_type=F32)
        carry_ref[...] += skip[0:1, :] + lk[0:1, :]

    @pl.when(j == 0)
    def _():
        acc_ref[...] = jnp.zeros_like(acc_ref)
        carry_ref[...] = jnp.zeros_like(carry_ref)
        key = lax.broadcasted_iota(jnp.int32, (page, ncol), 0)
        tok = lax.broadcasted_iota(jnp.int32, (page, ncol), 1) // n_heads
        block(kn_ref, vn_ref, key < tok)

    for k_ref, v_ref in zip(k_refs, v_refs):
        block(k_ref, v_ref, None)

    @pl.when(j == pl.num_programs(1) - 1)
    def _():
        acc = acc_ref[...]
        r = lax.broadcasted_iota(jnp.int32, acc.shape, 0) % n_heads
        c = lax.broadcasted_iota(jnp.int32, acc.shape, 1) // (acc.shape[1] // n_heads)
        own = jnp.where(r == c, acc, 0.0)
        o_ref[...] = own.reshape(n_tok, n_heads, acc.shape[1]).sum(axis=1)


def _attn_sample(q, k_new, v_new, cache_k, cache_v, page_table, layer, *, n_heads, d_head,
                 pages_per_step):
    bs, n_tok, wa = q.shape
    n_pages = page_table.shape[1]
    page = cache_k.shape[1]
    assert n_pages % pages_per_step == 0 and n_tok <= page
    steps = n_pages // pages_per_step
    ncol = n_tok * n_heads
    q4 = q.reshape(bs, n_tok, n_heads, d_head)
    qbd = jnp.einsum("bthd,hk->bkdth", q4, jnp.eye(n_heads, dtype=q.dtype)).reshape(bs, wa, ncol)
    pad = ((0, 0), (0, page - n_tok), (0, 0))
    kn = jnp.pad(k_new, pad)
    vn = jnp.pad(v_new, pad)
    base = layer

    def page_map(i):
        return lambda b, j, pt: (base + pt[b, n_pages - 1 - (j * pages_per_step + i)], 0, 0)

    page_specs = [pl.BlockSpec((None, page, wa), page_map(i)) for i in range(pages_per_step)]
    grid_spec = pltpu.PrefetchScalarGridSpec(
        num_scalar_prefetch=1,
        grid=(bs, steps),
        in_specs=[
            pl.BlockSpec((None, wa, ncol), lambda b, j, pt: (b, 0, 0)),
            pl.BlockSpec((None, page, wa), lambda b, j, pt: (b, 0, 0)),
            pl.BlockSpec((None, page, wa), lambda b, j, pt: (b, 0, 0)),
        ] + page_specs + page_specs,
        out_specs=pl.BlockSpec((None, n_tok, wa), lambda b, j, pt: (b, 0, 0)),
        scratch_shapes=[pltpu.VMEM((ncol, wa), F32), pltpu.VMEM((1, ncol), F32)],
    )
    return pl.pallas_call(
        functools.partial(_attn_sample_kernel, pages_per_step=pages_per_step, n_heads=n_heads,
                          n_tok=n_tok, scale=1.0 / math.sqrt(d_head)),
        grid_spec=grid_spec,
        out_shape=jax.ShapeDtypeStruct((bs, n_tok, wa), F32),
        compiler_params=_params(("parallel", "arbitrary")),
        name="attn_sample",
    )(page_table, qbd, kn, vn, *([cache_k] * pages_per_step), *([cache_v] * pages_per_step))


def _pool_kernel(halo_ref, u_ref, w_ref, sc_ref, o_ref, ext_ref, *, tp, pos0):
    t = pl.program_id(1)

    @pl.when(t == 0)
    def _():
        ext_ref[0:HALO, :] = halo_ref[...]

    ext_ref[HALO:HALO + tp, :] = u_ref[...]
    pos = pos0 + t * tp + lax.broadcasted_iota(jnp.int32, (tp, 1), 0)
    for g, win in enumerate(POOL_WINDOWS):
        sl = slice(g * LANES, (g + 1) * LANES)
        s = ext_ref[HALO:HALO + tp, sl]
        for i in range(1, win):
            s = s + ext_ref[HALO - i:HALO - i + tp, sl]
        cnt = jnp.minimum(pos + 1, win).astype(F32)
        diff = s / cnt - u_ref[:, sl]
        o_ref[:, sl] = _dot(diff.astype(BF16), w_ref[g].astype(BF16)) * sc_ref[:, sl]
    ext_ref[0:HALO, :] = ext_ref[tp:tp + HALO, :]


def _pool(u, halo, w, scale, *, tp, pos0):
    bsz, t_len, wp = u.shape
    assert t_len % tp == 0 and wp == len(POOL_WINDOWS) * LANES
    return pl.pallas_call(
        functools.partial(_pool_kernel, tp=tp, pos0=pos0),
        grid=(bsz, t_len // tp),
        in_specs=[
            pl.BlockSpec((None, HALO, wp), lambda b, t: (b, 0, 0)),
            pl.BlockSpec((None, tp, wp), lambda b, t: (b, t, 0)),
            pl.BlockSpec(w.shape, lambda b, t: (0, 0, 0)),
            pl.BlockSpec((1, wp), lambda b, t: (0, 0)),
        ],
        out_specs=pl.BlockSpec((None, tp, wp), lambda b, t: (b, t, 0)),
        out_shape=jax.ShapeDtypeStruct(u.shape, F32),
        scratch_shapes=[pltpu.VMEM((HALO + tp, wp), F32)],
        compiler_params=_params(("parallel", "arbitrary")),
        name="pool",
    )(halo, u, w, scale.reshape(1, wp))


def _ssm_prep_kernel(lr_ref, li_ref, ldt_ref, br_ref, bi_ref, lbr_ref, lbi_ref, bbr_ref, bbi_ref):
    lr = lr_ref[...]
    li = li_ref[...]
    dt = jnp.exp(ldt_ref[...])
    mag = jnp.exp(lr * dt)
    ang = li * dt
    lbr = mag * jnp.cos(ang)
    lbi = mag * jnp.sin(ang)
    nr = lbr - 1.0
    den = lr * lr + li * li
    fr = (nr * lr + lbi * li) / den
    fi = (lbi * lr - nr * li) / den
    br = br_ref[...]
    bi = bi_ref[...]
    lbr_ref[...] = lbr
    lbi_ref[...] = lbi
    bbr_ref[...] = fr * br - fi * bi
    bbi_ref[...] = fr * bi + fi * br


def _ssm_prep(lam_re, lam_im, log_dt, b_re, b_im):
    depth, g, p = lam_re.shape
    ch = b_re.shape[-1]
    shape = (depth, g, ch, p)
    bc = lambda a: jnp.broadcast_to(a[:, :, None, :], shape)
    spec = pl.BlockSpec((None, g, ch, p), lambda l: (l, 0, 0, 0))
    out = jax.ShapeDtypeStruct(shape, F32)
    lbr, lbi, bbr, bbi = pl.pallas_call(
        _ssm_prep_kernel,
        grid=(depth,),
        in_specs=[spec] * 5,
        out_specs=[spec] * 4,
        out_shape=[out] * 4,
        compiler_params=_params(("parallel",)),
        name="ssm_prep",
    )(bc(lam_re), bc(lam_im), jnp.broadcast_to(log_dt[:, :, None, None], shape),
      b_re.transpose(0, 1, 3, 2), b_im.transpose(0, 1, 3, 2))
    return lbr[:, :, 0, :], lbi[:, :, 0, :], bbr, bbi


def _gelu_glu(y, wglu_ref):
    g = jax.nn.gelu(y, approximate=True)
    return g * jax.nn.sigmoid(_dot(g.astype(BF16), wglu_ref[...].astype(BF16)))


def _ssm_prompt_kernel(u_ref, bh_ref, cre_ref, cim_ref, lre_ref, lim_ref, d_ref, wglu_ref,
                       y_ref, hre_ref, him_ref, buf_ref, *, lc, nb, half_in, half_out):
    c = pl.program_id(0)
    nslab = buf_ref.shape[0] // 2
    rows = 2 * nb

    @pl.when(c == 0)
    def _():
        hre_ref[...] = jnp.zeros_like(hre_ref)
        him_ref[...] = jnp.zeros_like(him_ref)

    for half in range(2):
        for b in range(nb):
            s = half * nb + b
            ub = u_ref[b, :, half * half_in:(half + 1) * half_in].astype(BF16)
            bu = _dot(ub, bh_ref[half])
            for j in range(2 * nslab):
                buf_ref[j, pl.ds(s, lc, stride=rows), :] = bu[:, j * LANES:(j + 1) * LANES]

    lam_r = [lre_ref[:, j * LANES:(j + 1) * LANES] for j in range(nslab)]
    lam_i = [lim_ref[:, j * LANES:(j + 1) * LANES] for j in range(nslab)]
    init = (tuple(hre_ref[:, j * LANES:(j + 1) * LANES] for j in range(nslab)),
            tuple(him_ref[:, j * LANES:(j + 1) * LANES] for j in range(nslab)))

    def step(k, carry):
        hr, hi = carry
        r0 = pl.multiple_of(k * rows, rows)
        new_r, new_i = [], []
        for j in range(nslab):
            xr = buf_ref[j, pl.ds(r0, rows), :]
            xi = buf_ref[nslab + j, pl.ds(r0, rows), :]
            nr = lam_r[j] * hr[j] - lam_i[j] * hi[j] + xr
            ni = lam_r[j] * hi[j] + lam_i[j] * hr[j] + xi
            buf_ref[j, pl.ds(r0, rows), :] = nr
            buf_ref[nslab + j, pl.ds(r0, rows), :] = ni
            new_r.append(nr)
            new_i.append(ni)
        return tuple(new_r), tuple(new_i)

    hr, hi = lax.fori_loop(0, lc, step, init, unroll=2)
    for j in range(nslab):
        hre_ref[:, j * LANES:(j + 1) * LANES] = hr[j]
        him_ref[:, j * LANES:(j + 1) * LANES] = hi[j]

    for b in range(nb):
        parts = []
        for half in range(2):
            s = half * nb + b
            h_re = jnp.concatenate(
                [buf_ref[j, pl.ds(s, lc, stride=rows), :] for j in range(nslab)], axis=1)
            h_im = jnp.concatenate(
                [buf_ref[nslab + j, pl.ds(s, lc, stride=rows), :] for j in range(nslab)], axis=1)
            parts.append(_dot(h_re.astype(BF16), cre_ref[half]) - _dot(h_im.astype(BF16), cim_ref[half]))
        y = jnp.concatenate(parts, axis=1) + d_ref[...] * u_ref[b]
        y_ref[b] = _gelu_glu(y, wglu_ref)


def _ssm_prompt(u, bh, cre, cim, lam_re8, lam_im8, d_skip, w_glu, *, lc):
    nb, t_len, w = u.shape
    s2 = lam_re8.shape[1]
    assert 2 * nb == SUBLANES and t_len % lc == 0 and s2 % LANES == 0
    nslab = s2 // LANES
    st = jax.ShapeDtypeStruct((2 * nb, s2), F32)
    full = lambda a: pl.BlockSpec(a.shape, lambda c: (0,) * a.ndim)
    d2 = d_skip.reshape(1, w)
    return pl.pallas_call(
        functools.partial(_ssm_prompt_kernel, lc=lc, nb=nb, half_in=w // 2, half_out=w // 2),
        grid=(t_len // lc,),
        in_specs=[pl.BlockSpec((nb, lc, w), lambda c: (0, c, 0)),
                  full(bh), full(cre), full(cim), full(lam_re8), full(lam_im8), full(d2), full(w_glu)],
        out_specs=[pl.BlockSpec((nb, lc, w), lambda c: (0, c, 0)),
                   pl.BlockSpec((2 * nb, s2), lambda c: (0, 0)),
                   pl.BlockSpec((2 * nb, s2), lambda c: (0, 0))],
        out_shape=[jax.ShapeDtypeStruct(u.shape, F32), st, st],
        scratch_shapes=[pltpu.VMEM((2 * nslab, lc * 2 * nb, LANES), F32)],
        compiler_params=_params(("arbitrary",)),
        name="ssm_prompt",
    )(u, bh, cre, cim, lam_re8, lam_im8, d2, w_glu)


def _ssm_sample_kernel(u_ref, h0r_ref, h0i_ref, bre_ref, bim_ref, cre_ref, cim_ref, lre_ref, lim_ref,
                       d_ref, wglu_ref, y_ref, hre_ref, him_ref, *, n_tok):
    hr = h0r_ref[...]
    hi = h0i_ref[...]
    lr = lre_ref[...]
    li = lim_ref[...]
    for t in range(n_tok):
        u = u_ref[t]
        ub = u.astype(BF16)
        hr, hi = (lr * hr - li * hi + _dot(ub, bre_ref[...]),
                  lr * hi + li * hr + _dot(ub, bim_ref[...]))
        y = (_dot(hr.astype(BF16), cre_ref[...]) - _dot(hi.astype(BF16), cim_ref[...])
             + d_ref[...] * u)
        y_ref[t] = _gelu_glu(y, wglu_ref)
    hre_ref[...] = hr
    him_ref[...] = hi


def _ssm_sample(u_t, h0_re, h0_im, bbd_re, bbd_im, cbd_re, cbd_im, lam_re, lam_im, d_skip, w_glu):
    n_tok, bs, w = u_t.shape
    s = h0_re.shape[1]
    st = jax.ShapeDtypeStruct((bs, s), F32)
    lam_b = lambda a: jnp.broadcast_to(a.reshape(1, s), (bs, s))
    return pl.pallas_call(
        functools.partial(_ssm_sample_kernel, n_tok=n_tok),
        out_shape=[jax.ShapeDtypeStruct(u_t.shape, F32), st, st],
        compiler_params=_params(None),
        name="ssm_sample",
    )(u_t, h0_re, h0_im, bbd_re, bbd_im, cbd_re, cbd_im, lam_b(lam_re), lam_b(lam_im),
      d_skip.reshape(1, w), w_glu)


def _block_diag_b(bb):
    depth, g, ch, p = bb.shape
    return jnp.einsum("lgcp,gh->lgchp", bb, jnp.eye(g, dtype=bb.dtype)).reshape(depth, g * ch, g * p)


def _block_diag_c(c):
    depth, g, ch, p = c.shape
    return jnp.einsum("lghp,gk->lgpkh", c, jnp.eye(g, dtype=c.dtype)).reshape(depth, g * p, g * ch)


def kernel(x_prompt, x_sample, cache_k, cache_v, state_ssm_re, state_ssm_im, state_pool, page_table,
           norm_ffn1, ffn1_w_gate, ffn1_w_up, ffn1_w_down, norm_mix, w_in, b_in, ssm_lambda_re,
           ssm_lambda_im, ssm_log_dt, ssm_b_re, ssm_b_im, ssm_c_re, ssm_c_im, ssm_d, ssm_w_glu,
           pool_w, pool_scale, norm_attn_out, norm_ssm_out, norm_pool_out, w_out, norm_ffn2,
           ffn2_w_gate, ffn2_w_up, ffn2_w_down, norm_final):
    bp, seq, d_model = x_prompt.shape
    bs, dec_seq, _ = x_sample.shape
    depth, n_pool, page, n_heads, d_head = cache_k.shape
    w_attn = n_heads * d_head
    _, n_groups, n_state = ssm_lambda_re.shape
    ssm_ch = ssm_b_re.shape[-1]
    w_ssm = n_groups * ssm_ch
    w_pool = pool_scale.shape[-1]
    pool_buf = state_pool.shape[2]
    past_len = page_table.shape[1] * page
    mp, ms = bp * seq, bs * dec_seq
    s_all = n_groups * n_state
    s_half = s_all // 2

    lb_re, lb_im, bb_re, bb_im = _ssm_prep(ssm_lambda_re, ssm_lambda_im, ssm_log_dt, ssm_b_re, ssm_b_im)
    bbd_re = _block_diag_b(bb_re).astype(BF16)
    bbd_im = _block_diag_b(bb_im).astype(BF16)
    cbd_re = _block_diag_c(ssm_c_re).astype(BF16)
    cbd_im = _block_diag_c(ssm_c_im).astype(BF16)
    lam_re = lb_re.reshape(depth, 1, s_all)
    lam_im = lb_im.reshape(depth, 1, s_all)
    hw = w_ssm // 2

    def halves_b(l):
        return jnp.stack([jnp.concatenate([bbd_re[l, h * hw:(h + 1) * hw, h * s_half:(h + 1) * s_half],
                                           bbd_im[l, h * hw:(h + 1) * hw, h * s_half:(h + 1) * s_half]],
                                          axis=1) for h in range(2)])

    def halves_c(cbd, l):
        return jnp.stack([cbd[l, h * s_half:(h + 1) * s_half, h * hw:(h + 1) * hw] for h in range(2)])

    def lam8(lam, l):
        return jnp.repeat(lam[l].reshape(2, s_half), bp, axis=0)

    cache_k2 = cache_k.reshape(depth * n_pool, page, w_attn)
    cache_v2 = cache_v.reshape(depth * n_pool, page, w_attn)
    zero_halo = jnp.zeros((bp, HALO, w_pool), F32)
    pad_t = SUBLANES - dec_seq

    xp = x_prompt.reshape(mp, d_model)
    xs = x_sample.reshape(ms, d_model)
    outs = {n: [] for n in ("kp", "vp", "hrp", "hip", "bufp", "ks", "vs", "hrs", "his", "bufs")}
    tm_p = 1024
    for l in range(depth):
        xp = _ffn(xp, norm_ffn1[l], ffn1_w_gate[l], ffn1_w_up[l], ffn1_w_down[l], norm_final,
                  final_norm=False, tm=tm_p, tf=256)
        q, k, v, us, up = _in_proj(xp, norm_mix[l], w_in[l], b_in[l], w_attn=w_attn, w_ssm=w_ssm,
                                   w_pool=w_pool, tm=tm_p)
        att = _attn_prompt(q, k, v, bsz=bp, t_len=seq, n_heads=n_heads, d_head=d_head, blk=256)
        ssm, st_re, st_im = _ssm_prompt(us.reshape(bp, seq, w_ssm), halves_b(l), halves_c(cbd_re, l),
                                        halves_c(cbd_im, l), lam8(lam_re, l), lam8(lam_im, l),
                                        ssm_d[l], ssm_w_glu[l], lc=128)
        up3 = up.reshape(bp, seq, w_pool)
        pool = _pool(up3, zero_halo, pool_w[l], pool_scale[l], tp=512, pos0=0)
        xp = _out_proj(xp, att, ssm.reshape(mp, w_ssm), pool.reshape(mp, w_pool), norm_attn_out[l],
                       norm_ssm_out[l], norm_pool_out[l], w_out[l], tm=tm_p, tn=512)
        xp = _ffn(xp, norm_ffn2[l], ffn2_w_gate[l], ffn2_w_up[l], ffn2_w_down[l], norm_final,
                  final_norm=(l == depth - 1), tm=tm_p, tf=256)
        outs["kp"].append(k.reshape(bp, seq, n_heads, d_head))
        outs["vp"].append(v.reshape(bp, seq, n_heads, d_head))
        unhalf = lambda a: a.reshape(2, bp, n_groups // 2, n_state).transpose(1, 0, 2, 3).reshape(
            bp, n_groups, n_state)
        outs["hrp"].append(unhalf(st_re))
        outs["hip"].append(unhalf(st_im))
        outs["bufp"].append(up3[:, seq - pool_buf:, :])

        xs = _ffn(xs, norm_ffn1[l], ffn1_w_gate[l], ffn1_w_up[l], ffn1_w_down[l], norm_final,
                  final_norm=False, tm=ms, tf=512)
        q, k, v, us, up = _in_proj(xs, norm_mix[l], w_in[l], b_in[l], w_attn=w_attn, w_ssm=w_ssm,
                                   w_pool=w_pool, tm=ms)
        att = _attn_sample(q.reshape(bs, dec_seq, w_attn), k.reshape(bs, dec_seq, w_attn),
                           v.reshape(bs, dec_seq, w_attn), cache_k2, cache_v2, page_table, l * n_pool,
                           n_heads=n_heads, d_head=d_head, pages_per_step=4)
        u_t = us.reshape(bs, dec_seq, w_ssm).transpose(1, 0, 2)
        ssm_t, st_re, st_im = _ssm_sample(u_t, state_ssm_re[l].reshape(bs, s_all),
                                          state_ssm_im[l].reshape(bs, s_all), bbd_re[l], bbd_im[l],
                                          cbd_re[l], cbd_im[l], lam_re[l], lam_im[l], ssm_d[l],
                                          ssm_w_glu[l])
        ssm = ssm_t.transpose(1, 0, 2).reshape(ms, w_ssm)
        up3 = up.reshape(bs, dec_seq, w_pool)
        halo = jnp.pad(state_pool[l], ((0, 0), (HALO - pool_buf, 0), (0, 0)))
        pool = _pool(jnp.pad(up3, ((0, 0), (0, pad_t), (0, 0))), halo, pool_w[l], pool_scale[l],
                     tp=SUBLANES, pos0=past_len)[:, :dec_seq]
        xs = _out_proj(xs, att.reshape(ms, w_attn), ssm, pool.reshape(ms, w_pool), norm_attn_out[l],
                       norm_ssm_out[l], norm_pool_out[l], w_out[l], tm=ms, tn=512)
        xs = _ffn(xs, norm_ffn2[l], ffn2_w_gate[l], ffn2_w_up[l], ffn2_w_down[l], norm_final,
                  final_norm=(l == depth - 1), tm=ms, tf=512)
        outs["ks"].append(k.reshape(bs, dec_seq, n_heads, d_head))
        outs["vs"].append(v.reshape(bs, dec_seq, n_heads, d_head))
        outs["hrs"].append(st_re.reshape(bs, n_groups, n_state))
        outs["his"].append(st_im.reshape(bs, n_groups, n_state))
        outs["bufs"].append(jnp.concatenate([state_pool[l], up3], axis=1)[:, -pool_buf:])

    st = lambda n: jnp.stack(outs[n])
    return (xp.reshape(bp, seq, d_model), xs.reshape(bs, dec_seq, d_model),
            st("kp"), st("vp"), st("hrp"), st("hip"), st("bufp"),
            st("ks"), st("vs"), st("hrs"), st("his"), st("bufs"))
```

```python
import functools
import math

import jax
import jax.numpy as jnp
from jax import lax
from jax.experimental import pallas as pl
from jax.experimental.pallas import tpu as pltpu

F32 = jnp.float32
BF16 = jnp.bfloat16
RMS_EPS = 1e-6
LANES = 128
SUBLANES = 8
VMEM_LIMIT = 56 * 1024 * 1024

POOL_WINDOWS = (2, 4, 8, 16)
HALO = 16


def _params(sem, vmem=VMEM_LIMIT):
    return pltpu.CompilerParams(dimension_semantics=sem, vmem_limit_bytes=vmem)


def _rms_scale(x):
    return x * lax.rsqrt(jnp.mean(x * x, axis=-1, keepdims=True) + RMS_EPS)


def _dot(a, b):
    return jnp.dot(a, b, preferred_element_type=F32)


def _ffn_kernel(x_ref, g_ref, wg_ref, wu_ref, wd_ref, gf_ref, o_ref, hn_ref, *, final_norm):
    f = pl.program_id(1)

    @pl.when(f == 0)
    def _():
        hn_ref[...] = (_rms_scale(x_ref[...]) * g_ref[...]).astype(BF16)
        o_ref[...] = jnp.zeros_like(o_ref)

    hn = hn_ref[...]
    gate = _dot(hn, wg_ref[...].astype(BF16))
    up = _dot(hn, wu_ref[...].astype(BF16))
    act = (gate * jax.nn.sigmoid(gate) * up).astype(BF16)
    o_ref[...] += _dot(act, wd_ref[...].astype(BF16))

    @pl.when(f == pl.num_programs(1) - 1)
    def _():
        y = x_ref[...] + 0.5 * o_ref[...]
        if final_norm:
            y = _rms_scale(y) * gf_ref[...]
        o_ref[...] = y


def _ffn(x, g, wg, wu, wd, g_final, *, final_norm, tm, tf):
    m, d = x.shape
    d_ff = wg.shape[1]
    assert m % tm == 0 and d_ff % tf == 0
    return pl.pallas_call(
        functools.partial(_ffn_kernel, final_norm=final_norm),
        grid=(m // tm, d_ff // tf),
        in_specs=[
            pl.BlockSpec((tm, d), lambda i, f: (i, 0), pipeline_mode=pl.Buffered(1)),
            pl.BlockSpec((1, d), lambda i, f: (0, 0)),
            pl.BlockSpec((d, tf), lambda i, f: (0, f)),
            pl.BlockSpec((d, tf), lambda i, f: (0, f)),
            pl.BlockSpec((tf, d), lambda i, f: (f, 0)),
            pl.BlockSpec((1, d), lambda i, f: (0, 0)),
        ],
        out_specs=pl.BlockSpec((tm, d), lambda i, f: (i, 0)),
        out_shape=jax.ShapeDtypeStruct((m, d), F32),
        scratch_shapes=[pltpu.VMEM((tm, d), BF16)],
        compiler_params=_params(("parallel", "arbitrary")),
        name="ffn",
    )(x, g.reshape(1, d), wg, wu, wd, g_final.reshape(1, d))


def _in_proj_kernel(x_ref, g_ref, w_ref, b_ref, k_all_ref, v_all_ref, q_ref, kb_ref, vb_ref, k_ref,
                    v_ref, us_ref, up_ref, hn_ref, *, na, n_heads):
    del k_all_ref, v_all_ref
    n = pl.program_id(1)
    tm = x_ref.shape[0]
    hpt = w_ref.shape[1] // LANES

    @pl.when(n == 0)
    def _():
        hn_ref[...] = (_rms_scale(x_ref[...]) * g_ref[...]).astype(BF16)

    y = _dot(hn_ref[...], w_ref[...].astype(BF16)) + b_ref[...]

    def put_heads(dst_ref, c):
        for j in range(hpt):
            dst_ref[pl.ds(c * hpt + j, tm, stride=n_heads), :] = y[:, j * LANES:(j + 1) * LANES]

    @pl.when(n < na)
    def _():
        q_ref[...] = y.astype(BF16)

    for c in range(na):
        @pl.when(n == na + c)
        def _():
            kb_ref[...] = y.astype(BF16)
            put_heads(k_ref, c)

        @pl.when(n == 2 * na + c)
        def _():
            vb_ref[...] = y.astype(BF16)
            put_heads(v_ref, c)

    @pl.when(n == 3 * na)
    def _():
        us_ref[...] = y

    @pl.when(n == 3 * na + 1)
    def _():
        up_ref[...] = y


def _in_proj(x, g, w, b, k_all, v_all, layer, *, n_heads, w_ssm, w_pool, tm):
    m, d = x.shape
    w_attn = n_heads * k_all.shape[2]
    tn = w_ssm
    assert w_pool == tn and w_attn % tn == 0 and m % tm == 0 and k_all.shape[1] == m * n_heads
    assert k_all.shape[2] == LANES
    na = w_attn // tn
    n_steps = 3 * na + 2
    assert w.shape[1] == n_steps * tn

    def attn_spec(first):
        return pl.BlockSpec((tm, tn), lambda i, n: (i, jnp.clip(n - first, 0, na - 1)))

    rows_spec = pl.BlockSpec((None, tm * n_heads, LANES), lambda i, n: (layer, i, 0))
    attn_bf16 = jax.ShapeDtypeStruct((m, w_attn), BF16)
    return pl.pallas_call(
        functools.partial(_in_proj_kernel, na=na, n_heads=n_heads),
        grid=(m // tm, n_steps),
        in_specs=[
            pl.BlockSpec((tm, d), lambda i, n: (i, 0), pipeline_mode=pl.Buffered(1)),
            pl.BlockSpec((1, d), lambda i, n: (0, 0)),
            pl.BlockSpec((d, tn), lambda i, n: (0, n)),
            pl.BlockSpec((1, tn), lambda i, n: (0, n)),
            pl.BlockSpec(memory_space=pl.ANY),
            pl.BlockSpec(memory_space=pl.ANY),
        ],
        out_specs=[
            attn_spec(0), attn_spec(na), attn_spec(2 * na), rows_spec, rows_spec,
            pl.BlockSpec((tm, tn), lambda i, n: (i, 0)),
            pl.BlockSpec((tm, tn), lambda i, n: (i, 0)),
        ],
        out_shape=[
            attn_bf16, attn_bf16, attn_bf16,
            jax.ShapeDtypeStruct(k_all.shape, F32),
            jax.ShapeDtypeStruct(v_all.shape, F32),
            jax.ShapeDtypeStruct((m, w_ssm), F32),
            jax.ShapeDtypeStruct((m, w_pool), F32),
        ],
        input_output_aliases={4: 3, 5: 4},
        scratch_shapes=[pltpu.VMEM((tm, d), BF16)],
        compiler_params=_params(("parallel", "arbitrary")),
        name="in_proj",
    )(x, g.reshape(1, d), w, b.reshape(1, -1), k_all, v_all)


def _out_proj_kernel(x_ref, a_ref, s_ref, p_ref, ga_ref, gs_ref, gp_ref, w_ref, o_ref, m_ref,
                     *, w_attn, w_ssm):
    n = pl.program_id(1)

    @pl.when(n == 0)
    def _():
        m_ref[:, 0:w_attn] = (_rms_scale(a_ref[...]) * ga_ref[...]).astype(BF16)
        m_ref[:, w_attn:w_attn + w_ssm] = (_rms_scale(s_ref[...]) * gs_ref[...]).astype(BF16)
        m_ref[:, w_attn + w_ssm:] = (_rms_scale(p_ref[...]) * gp_ref[...]).astype(BF16)

    o_ref[...] = x_ref[...] + _dot(m_ref[...], w_ref[...].astype(BF16))


def _out_proj(x, att, ssm, pool, ga, gs, gp, w, *, tm, tn):
    m, d = x.shape
    wa, ws, wp = att.shape[1], ssm.shape[1], pool.shape[1]
    mix = wa + ws + wp
    assert w.shape == (mix, d) and m % tm == 0 and d % tn == 0
    return pl.pallas_call(
        functools.partial(_out_proj_kernel, w_attn=wa, w_ssm=ws),
        grid=(m // tm, d // tn),
        in_specs=[
            pl.BlockSpec((tm, tn), lambda i, n: (i, n)),
            pl.BlockSpec((tm, wa), lambda i, n: (i, 0)),
            pl.BlockSpec((tm, ws), lambda i, n: (i, 0)),
            pl.BlockSpec((tm, wp), lambda i, n: (i, 0)),
            pl.BlockSpec((1, wa), lambda i, n: (0, 0)),
            pl.BlockSpec((1, ws), lambda i, n: (0, 0)),
            pl.BlockSpec((1, wp), lambda i, n: (0, 0)),
            pl.BlockSpec((mix, tn), lambda i, n: (0, n)),
        ],
        out_specs=pl.BlockSpec((tm, tn), lambda i, n: (i, n)),
        out_shape=jax.ShapeDtypeStruct((m, d), F32),
        scratch_shapes=[pltpu.VMEM((tm, mix), BF16)],
        compiler_params=_params(("parallel", "arbitrary")),
        name="out_proj",
    )(x, att, ssm, pool, ga.reshape(1, wa), gs.reshape(1, ws), gp.reshape(1, wp), w)


def _sb_terms(z):
    ls = jnp.minimum(z, 0.0) - jnp.log(1.0 + jnp.exp(-jnp.abs(z)))
    return ls, ls - z


def _split_bf16(x):
    hi = x.astype(BF16)
    lo = (x - hi.astype(F32)).astype(BF16)
    return hi, lo


def _sb_log_weights(z, after, mask):
    ls, lk = _sb_terms(z)
    if mask is not None:
        lk = jnp.where(mask, lk, 0.0)
    hi, lo = _split_bf16(lk)
    skip = _dot(hi, after) + _dot(lo, after)
    return ls + skip, skip[:, 0:1] + lk[:, 0:1]


def _sb_weights(log_w, carry, mask):
    w = jnp.exp(log_w + carry)
    if mask is not None:
        w = jnp.where(mask, w, 0.0)
    return w.astype(BF16)


def _after_matrix(n):
    row = lax.broadcasted_iota(jnp.int32, (n, n), 0)
    col = lax.broadcasted_iota(jnp.int32, (n, n), 1)
    return jnp.where(row > col, 1.0, 0.0).astype(BF16), col < row


def _attn_prompt_kernel(q_ref, k_ref, v_ref, o_ref, carry_ref, *, blk, hp, scale):
    qi = pl.program_id(2)
    after, causal = _after_matrix(blk)

    def block(start, mask):
        sls = [slice(hh * LANES, (hh + 1) * LANES) for hh in range(hp)]
        zs = [lax.dot_general(q_ref[:, sl], k_ref[pl.ds(start, blk), sl], (((1,), (1,)), ((), ())),
                              preferred_element_type=F32) * scale for sl in sls]
        lws = [_sb_log_weights(z, after, mask) for z in zs]
        for hh, (sl, (log_w, total)) in enumerate(zip(sls, lws)):
            w = _sb_weights(log_w, carry_ref[hh], mask)
            o_ref[:, sl] += _dot(w, v_ref[pl.ds(start, blk), sl])
            carry_ref[hh] += total

    o_ref[...] = jnp.zeros_like(o_ref)
    carry_ref[...] = jnp.zeros_like(carry_ref)
    block(pl.multiple_of(qi * blk, blk), causal)

    def body(i, c):
        block(pl.multiple_of((qi - 1 - i) * blk, blk), None)
        return c

    lax.fori_loop(0, qi, body, 0)


def _attn_prompt(q, k, v, *, bsz, t_len, n_heads, d_head, blk, hp):
    assert t_len % blk == 0 and d_head == LANES and n_heads % hp == 0
    nq = t_len // blk
    kv_spec = pl.BlockSpec((t_len, hp * d_head), lambda b, h, i: (b, h))
    return pl.pallas_call(
        functools.partial(_attn_prompt_kernel, blk=blk, hp=hp, scale=1.0 / math.sqrt(d_head)),
        grid=(bsz, n_heads // hp, nq),
        in_specs=[pl.BlockSpec((blk, hp * d_head), lambda b, h, i: (b * nq + i, h)), kv_spec, kv_spec],
        out_specs=pl.BlockSpec((blk, hp * d_head), lambda b, h, i: (b * nq + i, h)),
        out_shape=jax.ShapeDtypeStruct(q.shape, F32),
        scratch_shapes=[pltpu.VMEM((hp, blk, 1), F32)],
        compiler_params=_params(("parallel", "parallel", "arbitrary")),
        name="attn_prompt",
    )(q, k, v)


def _attn_sample_kernel(pt_ref, q_ref, kn_ref, vn_ref, *rest, pages_per_step, n_heads, tq, scale):
    del pt_ref
    k_refs = rest[:pages_per_step]
    v_refs = rest[pages_per_step:2 * pages_per_step]
    o_ref, carry_ref = rest[2 * pages_per_step:]
    j = pl.program_id(1)
    page = k_refs[0].shape[0] // n_heads
    after, _ = _after_matrix(page)

    def head_rows(ref, h):
        return ref[pl.ds(h, page, stride=n_heads), :].astype(BF16)

    def log_weights(k_ref, mask):
        zs = [lax.dot_general(q_ref[h], head_rows(k_ref, h), (((1,), (1,)), ((), ())),
                              preferred_element_type=F32) for h in range(n_heads)]
        z = jnp.concatenate(zs, axis=0) * scale
        return _sb_log_weights(z, after, mask)

    def walk(kv_refs, mask):
        lws = [log_weights(k_ref, mask) for k_ref, _ in kv_refs]
        carry = carry_ref[...]
        ws = []
        for log_w, total in lws:
            ws.append(_sb_weights(log_w, carry, mask))
            carry = carry + total
        carry_ref[...] = carry
        for h in range(n_heads):
            rows = slice(h * tq, (h + 1) * tq)
            acc = o_ref[rows, :]
            for w, (_, v_ref) in zip(ws, kv_refs):
                acc = acc + _dot(w[rows, :], head_rows(v_ref, h))
            o_ref[rows, :] = acc

    @pl.when(j == 0)
    def _():
        o_ref[...] = jnp.zeros_like(o_ref)
        carry_ref[...] = jnp.zeros_like(carry_ref)
        key = lax.broadcasted_iota(jnp.int32, (n_heads * tq, page), 1)
        tok = lax.broadcasted_iota(jnp.int32, (n_heads * tq, page), 0) % tq
        walk([(kn_ref, vn_ref)], key < tok)

    walk(list(zip(k_refs, v_refs)), None)


def _attn_sample(q, k_new, v_new, cache_k, cache_v, page_table, layer, *, n_heads, d_head,
                 pages_per_step):
    bs, n_tok, wa = q.shape
    n_pages = page_table.shape[1]
    rows = cache_k.shape[1]
    tq = 2 * SUBLANES
    assert n_pages % pages_per_step == 0 and n_tok <= tq and d_head == LANES
    steps = n_pages // pages_per_step
    q4 = q.reshape(bs, n_tok, n_heads, d_head).transpose(0, 2, 1, 3)
    q4 = jnp.pad(q4, ((0, 0), (0, 0), (0, tq - n_tok), (0, 0)))
    pad = ((0, 0), (0, rows - n_tok * n_heads), (0, 0))
    kn = jnp.pad(k_new, pad)
    vn = jnp.pad(v_new, pad)

    def page_map(i):
        return lambda b, j, pt: (layer + pt[b, n_pages - 1 - (j * pages_per_step + i)], 0, 0)

    page_specs = [pl.BlockSpec((None, rows, d_head), page_map(i)) for i in range(pages_per_step)]
    grid_spec = pltpu.PrefetchScalarGridSpec(
        num_scalar_prefetch=1,
        grid=(bs, steps),
        in_specs=[
            pl.BlockSpec((None, n_heads, tq, d_head), lambda b, j, pt: (b, 0, 0, 0)),
            pl.BlockSpec((None, rows, d_head), lambda b, j, pt: (b, 0, 0)),
            pl.BlockSpec((None, rows, d_head), lambda b, j, pt: (b, 0, 0)),
        ] + page_specs + page_specs,
        out_specs=pl.BlockSpec((None, n_heads * tq, d_head), lambda b, j, pt: (b, 0, 0)),
        scratch_shapes=[pltpu.VMEM((n_heads * tq, 1), F32)],
    )
    out = pl.pallas_call(
        functools.partial(_attn_sample_kernel, pages_per_step=pages_per_step, n_heads=n_heads,
                          tq=tq, scale=1.0 / math.sqrt(d_head)),
        grid_spec=grid_spec,
        out_shape=jax.ShapeDtypeStruct((bs, n_heads * tq, d_head), F32),
        compiler_params=_params(("parallel", "arbitrary")),
        name="attn_sample",
    )(page_table, q4, kn, vn, *([cache_k] * pages_per_step), *([cache_v] * pages_per_step))
    out = out.reshape(bs, n_heads, tq, d_head)[:, :, :n_tok]
    return out.transpose(0, 2, 1, 3).reshape(bs, n_tok, wa)


def _pool_kernel(halo_ref, u_ref, w_ref, sc_ref, o_ref, ext_ref, *, tp, pos0):
    t = pl.program_id(1)

    @pl.when(t == 0)
    def _():
        ext_ref[0:HALO, :] = halo_ref[...]

    ext_ref[HALO:HALO + tp, :] = u_ref[...]
    pos = pos0 + t * tp + lax.broadcasted_iota(jnp.int32, (tp, 1), 0)
    for g, win in enumerate(POOL_WINDOWS):
        sl = slice(g * LANES, (g + 1) * LANES)
        s = ext_ref[HALO:HALO + tp, sl]
        for i in range(1, win):
            s = s + ext_ref[HALO - i:HALO - i + tp, sl]
        cnt = jnp.minimum(pos + 1, win).astype(F32)
        diff = s / cnt - u_ref[:, sl]
        o_ref[:, sl] = _dot(diff.astype(BF16), w_ref[g].astype(BF16)) * sc_ref[:, sl]
    ext_ref[0:HALO, :] = ext_ref[tp:tp + HALO, :]


def _pool(u, halo, w, scale, *, tp, pos0):
    bsz, t_len, wp = u.shape
    assert t_len % tp == 0 and wp == len(POOL_WINDOWS) * LANES
    return pl.pallas_call(
        functools.partial(_pool_kernel, tp=tp, pos0=pos0),
        grid=(bsz, t_len // tp),
        in_specs=[
            pl.BlockSpec((None, HALO, wp), lambda b, t: (b, 0, 0)),
            pl.BlockSpec((None, tp, wp), lambda b, t: (b, t, 0)),
            pl.BlockSpec(w.shape, lambda b, t: (0, 0, 0)),
            pl.BlockSpec((1, wp), lambda b, t: (0, 0)),
        ],
        out_specs=pl.BlockSpec((None, tp, wp), lambda b, t: (b, t, 0)),
        out_shape=jax.ShapeDtypeStruct(u.shape, F32),
        scratch_shapes=[pltpu.VMEM((HALO + tp, wp), F32)],
        compiler_params=_params(("parallel", "arbitrary")),
        name="pool",
    )(halo, u, w, scale.reshape(1, wp))


def _ssm_prep_kernel(lr_ref, li_ref, ldt_ref, br_ref, bi_ref, lbr_ref, lbi_ref, bbr_ref, bbi_ref):
    lr = lr_ref[...]
    li = li_ref[...]
    dt = jnp.exp(ldt_ref[...])
    mag = jnp.exp(lr * dt)
    ang = li * dt
    lbr = mag * jnp.cos(ang)
    lbi = mag * jnp.sin(ang)
    nr = lbr - 1.0
    den = lr * lr + li * li
    fr = (nr * lr + lbi * li) / den
    fi = (lbi * lr - nr * li) / den
    br = br_ref[...]
    bi = bi_ref[...]
    lbr_ref[...] = lbr
    lbi_ref[...] = lbi
    bbr_ref[...] = fr * br - fi * bi
    bbi_ref[...] = fr * bi + fi * br


def _ssm_prep(lam_re, lam_im, log_dt, b_re, b_im):
    depth, g, p = lam_re.shape
    ch = b_re.shape[-1]
    shape = (depth, g, ch, p)
    bc = lambda a: jnp.broadcast_to(a[:, :, None, :], shape)
    spec = pl.BlockSpec((None, g, ch, p), lambda l: (l, 0, 0, 0))
    out = jax.ShapeDtypeStruct(shape, F32)
    lbr, lbi, bbr, bbi = pl.pallas_call(
        _ssm_prep_kernel,
        grid=(depth,),
        in_specs=[spec] * 5,
        out_specs=[spec] * 4,
        out_shape=[out] * 4,
        compiler_params=_params(("parallel",)),
        name="ssm_prep",
    )(bc(lam_re), bc(lam_im), jnp.broadcast_to(log_dt[:, :, None, None], shape),
      b_re.transpose(0, 1, 3, 2), b_im.transpose(0, 1, 3, 2))
    return lbr[:, :, 0, :], lbi[:, :, 0, :], bbr, bbi


def _gelu_glu(y, wglu_ref):
    g = jax.nn.gelu(y, approximate=True)
    return g * jax.nn.sigmoid(_dot(g.astype(BF16), wglu_ref[...].astype(BF16)))


def _ssm_prompt_kernel(u_ref, bh_ref, cre_ref, cim_ref, lre_ref, lim_ref, d_ref, wglu_ref,
                       y_ref, hre_ref, him_ref, buf_ref, *, lc, nb, half_in, half_out):
    c = pl.program_id(0)
    nslab = buf_ref.shape[0] // 2
    rows = 2 * nb

    @pl.when(c == 0)
    def _():
        hre_ref[...] = jnp.zeros_like(hre_ref)
        him_ref[...] = jnp.zeros_like(him_ref)

    for half in range(2):
        for b in range(nb):
            s = half * nb + b
            ub = u_ref[b, :, half * half_in:(half + 1) * half_in].astype(BF16)
            bu = _dot(ub, bh_ref[half])
            for j in range(2 * nslab):
                buf_ref[j, pl.ds(s, lc, stride=rows), :] = bu[:, j * LANES:(j + 1) * LANES]

    lam_r = [lre_ref[:, j * LANES:(j + 1) * LANES] for j in range(nslab)]
    lam_i = [lim_ref[:, j * LANES:(j + 1) * LANES] for j in range(nslab)]
    init = (tuple(hre_ref[:, j * LANES:(j + 1) * LANES] for j in range(nslab)),
            tuple(him_ref[:, j * LANES:(j + 1) * LANES] for j in range(nslab)))

    def step(k, carry):
        hr, hi = carry
        r0 = pl.multiple_of(k * rows, rows)
        new_r, new_i = [], []
        for j in range(nslab):
            xr = buf_ref[j, pl.ds(r0, rows), :]
            xi = buf_ref[nslab + j, pl.ds(r0, rows), :]
            nr = lam_r[j] * hr[j] - lam_i[j] * hi[j] + xr
            ni = lam_r[j] * hi[j] + lam_i[j] * hr[j] + xi
            buf_ref[j, pl.ds(r0, rows), :] = nr
            buf_ref[nslab + j, pl.ds(r0, rows), :] = ni
            new_r.append(nr)
            new_i.append(ni)
        return tuple(new_r), tuple(new_i)

    hr, hi = lax.fori_loop(0, lc, step, init, unroll=2)
    for j in range(nslab):
        hre_ref[:, j * LANES:(j + 1) * LANES] = hr[j]
        him_ref[:, j * LANES:(j + 1) * LANES] = hi[j]

    for b in range(nb):
        parts = []
        for half in range(2):
            s = half * nb + b
            h_re = jnp.concatenate(
                [buf_ref[j, pl.ds(s, lc, stride=rows), :] for j in range(nslab)], axis=1)
            h_im = jnp.concatenate(
                [buf_ref[nslab + j, pl.ds(s, lc, stride=rows), :] for j in range(nslab)], axis=1)
            parts.append(_dot(h_re.astype(BF16), cre_ref[half]) - _dot(h_im.astype(BF16), cim_ref[half]))
        y = jnp.concatenate(parts, axis=1) + d_ref[...] * u_ref[b]
        y_ref[b] = _gelu_glu(y, wglu_ref)


def _ssm_prompt(u, bh, cre, cim, lam_re8, lam_im8, d_skip, w_glu, *, lc):
    nb, t_len, w = u.shape
    s2 = lam_re8.shape[1]
    assert 2 * nb == SUBLANES and t_len % lc == 0 and s2 % LANES == 0
    nslab = s2 // LANES
    st = jax.ShapeDtypeStruct((2 * nb, s2), F32)
    full = lambda a: pl.BlockSpec(a.shape, lambda c: (0,) * a.ndim)
    d2 = d_skip.reshape(1, w)
    return pl.pallas_call(
        functools.partial(_ssm_prompt_kernel, lc=lc, nb=nb, half_in=w // 2, half_out=w // 2),
        grid=(t_len // lc,),
        in_specs=[pl.BlockSpec((nb, lc, w), lambda c: (0, c, 0)),
                  full(bh), full(cre), full(cim), full(lam_re8), full(lam_im8), full(d2), full(w_glu)],
        out_specs=[pl.BlockSpec((nb, lc, w), lambda c: (0, c, 0)),
                   pl.BlockSpec((2 * nb, s2), lambda c: (0, 0)),
                   pl.BlockSpec((2 * nb, s2), lambda c: (0, 0))],
        out_shape=[jax.ShapeDtypeStruct(u.shape, F32), st, st],
        scratch_shapes=[pltpu.VMEM((2 * nslab, lc * 2 * nb, LANES), F32)],
        compiler_params=_params(("arbitrary",)),
        name="ssm_prompt",
    )(u, bh, cre, cim, lam_re8, lam_im8, d2, w_glu)


def _ssm_sample_kernel(u_ref, h0r_ref, h0i_ref, bre_ref, bim_ref, cre_ref, cim_ref, lre_ref, lim_ref,
                       d_ref, wglu_ref, y_ref, hre_ref, him_ref, *, n_tok):
    hr = h0r_ref[...]
    hi = h0i_ref[...]
    lr = lre_ref[...]
    li = lim_ref[...]
    for t in range(n_tok):
        u = u_ref[t]
        ub = u.astype(BF16)
        hr, hi = (lr * hr - li * hi + _dot(ub, bre_ref[...]),
                  lr * hi + li * hr + _dot(ub, bim_ref[...]))
        y = (_dot(hr.astype(BF16), cre_ref[...]) - _dot(hi.astype(BF16), cim_ref[...])
             + d_ref[...] * u)
        y_ref[t] = _gelu_glu(y, wglu_ref)
    hre_ref[...] = hr
    him_ref[...] = hi


def _ssm_sample(u_t, h0_re, h0_im, bbd_re, bbd_im, cbd_re, cbd_im, lam_re, lam_im, d_skip, w_glu):
    n_tok, bs, w = u_t.shape
    s = h0_re.shape[1]
    st = jax.ShapeDtypeStruct((bs, s), F32)
    lam_b = lambda a: jnp.broadcast_to(a.reshape(1, s), (bs, s))
    return pl.pallas_call(
        functools.partial(_ssm_sample_kernel, n_tok=n_tok),
        out_shape=[jax.ShapeDtypeStruct(u_t.shape, F32), st, st],
        compiler_params=_params(None),
        name="ssm_sample",
    )(u_t, h0_re, h0_im, bbd_re, bbd_im, cbd_re, cbd_im, lam_b(lam_re), lam_b(lam_im),
      d_skip.reshape(1, w), w_glu)


def _block_diag_b(bb):
    depth, g, ch, p = bb.shape
    return jnp.einsum("lgcp,gh->lgchp", bb, jnp.eye(g, dtype=bb.dtype)).reshape(depth, g * ch, g * p)


def _block_diag_c(c):
    depth, g, ch, p = c.shape
    return jnp.einsum("lghp,gk->lgpkh", c, jnp.eye(g, dtype=c.dtype)).reshape(depth, g * p, g * ch)


def kernel(x_prompt, x_sample, cache_k, cache_v, state_ssm_re, state_ssm_im, state_pool, page_table,
           norm_ffn1, ffn1_w_gate, ffn1_w_up, ffn1_w_down, norm_mix, w_in, b_in, ssm_lambda_re,
           ssm_lambda_im, ssm_log_dt, ssm_b_re, ssm_b_im, ssm_c_re, ssm_c_im, ssm_d, ssm_w_glu,
           pool_w, pool_scale, norm_attn_out, norm_ssm_out, norm_pool_out, w_out, norm_ffn2,
           ffn2_w_gate, ffn2_w_up, ffn2_w_down, norm_final):
    bp, seq, d_model = x_prompt.shape
    bs, dec_seq, _ = x_sample.shape
    depth, n_pool, page, n_heads, d_head = cache_k.shape
    w_attn = n_heads * d_head
    _, n_groups, n_state = ssm_lambda_re.shape
    ssm_ch = ssm_b_re.shape[-1]
    w_ssm = n_groups * ssm_ch
    w_pool = pool_scale.shape[-1]
    pool_buf = state_pool.shape[2]
    past_len = page_table.shape[1] * page
    mp, ms = bp * seq, bs * dec_seq
    s_all = n_groups * n_state
    s_half = s_all // 2

    lb_re, lb_im, bb_re, bb_im = _ssm_prep(ssm_lambda_re, ssm_lambda_im, ssm_log_dt, ssm_b_re, ssm_b_im)
    bbd_re = _block_diag_b(bb_re).astype(BF16)
    bbd_im = _block_diag_b(bb_im).astype(BF16)
    cbd_re = _block_diag_c(ssm_c_re).astype(BF16)
    cbd_im = _block_diag_c(ssm_c_im).astype(BF16)
    lam_re = lb_re.reshape(depth, 1, s_all)
    lam_im = lb_im.reshape(depth, 1, s_all)
    hw = w_ssm // 2

    def halves_b(l):
        return jnp.stack([jnp.concatenate([bbd_re[l, h * hw:(h + 1) * hw, h * s_half:(h + 1) * s_half],
                                           bbd_im[l, h * hw:(h + 1) * hw, h * s_half:(h + 1) * s_half]],
                                          axis=1) for h in range(2)])

    def halves_c(cbd, l):
        return jnp.stack([cbd[l, h * s_half:(h + 1) * s_half, h * hw:(h + 1) * hw] for h in range(2)])

    def lam8(lam, l):
        return jnp.repeat(lam[l].reshape(2, s_half), bp, axis=0)

    cache_k2 = cache_k.reshape(depth * n_pool, page * n_heads, d_head)
    cache_v2 = cache_v.reshape(depth * n_pool, page * n_heads, d_head)
    zero_halo = jnp.zeros((bp, HALO, w_pool), F32)
    pad_t = SUBLANES - dec_seq

    xp = x_prompt.reshape(mp, d_model)
    xs = x_sample.reshape(ms, d_model)
    kp_all = jnp.zeros((depth, mp * n_heads, d_head), F32)
    vp_all = jnp.zeros((depth, mp * n_heads, d_head), F32)
    ks_all = jnp.zeros((depth, ms * n_heads, d_head), F32)
    vs_all = jnp.zeros((depth, ms * n_heads, d_head), F32)
    outs = {n: [] for n in ("hrp", "hip", "bufp", "hrs", "his", "bufs")}
    tm_p = 1024
    for l in range(depth):
        xp = _ffn(xp, norm_ffn1[l], ffn1_w_gate[l], ffn1_w_up[l], ffn1_w_down[l], norm_final,
                  final_norm=False, tm=tm_p, tf=256)
        q, kb, vb, kp_all, vp_all, us, up = _in_proj(
            xp, norm_mix[l], w_in[l], b_in[l], kp_all, vp_all, l, n_heads=n_heads, w_ssm=w_ssm,
            w_pool=w_pool, tm=tm_p)
        att = _attn_prompt(q, kb, vb, bsz=bp, t_len=seq, n_heads=n_heads, d_head=d_head, blk=256,
                           hp=4)
        ssm, st_re, st_im = _ssm_prompt(us.reshape(bp, seq, w_ssm), halves_b(l), halves_c(cbd_re, l),
                                        halves_c(cbd_im, l), lam8(lam_re, l), lam8(lam_im, l),
                                        ssm_d[l], ssm_w_glu[l], lc=128)
        up3 = up.reshape(bp, seq, w_pool)
        pool = _pool(up3, zero_halo, pool_w[l], pool_scale[l], tp=512, pos0=0)
        xp = _out_proj(xp, att, ssm.reshape(mp, w_ssm), pool.reshape(mp, w_pool), norm_attn_out[l],
                       norm_ssm_out[l], norm_pool_out[l], w_out[l], tm=tm_p, tn=512)
        xp = _ffn(xp, norm_ffn2[l], ffn2_w_gate[l], ffn2_w_up[l], ffn2_w_down[l], norm_final,
                  final_norm=(l == depth - 1), tm=tm_p, tf=256)
        unhalf = lambda a: a.reshape(2, bp, n_groups // 2, n_state).transpose(1, 0, 2, 3).reshape(
            bp, n_groups, n_state)
        outs["hrp"].append(unhalf(st_re))
        outs["hip"].append(unhalf(st_im))
        outs["bufp"].append(up3[:, seq - pool_buf:, :])

        xs = _ffn(xs, norm_ffn1[l], ffn1_w_gate[l], ffn1_w_up[l], ffn1_w_down[l], norm_final,
                  final_norm=False, tm=ms, tf=512)
        q, _, _, ks_all, vs_all, us, up = _in_proj(
            xs, norm_mix[l], w_in[l], b_in[l], ks_all, vs_all, l, n_heads=n_heads, w_ssm=w_ssm,
            w_pool=w_pool, tm=ms)
        att = _attn_sample(q.reshape(bs, dec_seq, w_attn),
                           ks_all[l].reshape(bs, dec_seq * n_heads, d_head),
                           vs_all[l].reshape(bs, dec_seq * n_heads, d_head), cache_k2, cache_v2,
                           page_table, l * n_pool, n_heads=n_heads, d_head=d_head, pages_per_step=8)
        u_t = us.reshape(bs, dec_seq, w_ssm).transpose(1, 0, 2)
        ssm_t, st_re, st_im = _ssm_sample(u_t, state_ssm_re[l].reshape(bs, s_all),
                                          state_ssm_im[l].reshape(bs, s_all), bbd_re[l], bbd_im[l],
                                          cbd_re[l], cbd_im[l], lam_re[l], lam_im[l], ssm_d[l],
                                          ssm_w_glu[l])
        ssm = ssm_t.transpose(1, 0, 2).reshape(ms, w_ssm)
        up3 = up.reshape(bs, dec_seq, w_pool)
        halo = jnp.pad(state_pool[l], ((0, 0), (HALO - pool_buf, 0), (0, 0)))
        pool = _pool(jnp.pad(up3, ((0, 0), (0, pad_t), (0, 0))), halo, pool_w[l], pool_scale[l],
                     tp=SUBLANES, pos0=past_len)[:, :dec_seq]
        xs = _out_proj(xs, att.reshape(ms, w_attn), ssm, pool.reshape(ms, w_pool), norm_attn_out[l],
                       norm_ssm_out[l], norm_pool_out[l], w_out[l], tm=ms, tn=512)
        xs = _ffn(xs, norm_ffn2[l], ffn2_w_gate[l], ffn2_w_up[l], ffn2_w_down[l], norm_final,
                  final_norm=(l == depth - 1), tm=ms, tf=512)
        outs["hrs"].append(st_re.reshape(bs, n_groups, n_state))
        outs["his"].append(st_im.reshape(bs, n_groups, n_state))
        outs["bufs"].append(jnp.concatenate([state_pool[l], up3], axis=1)[:, -pool_buf:])

    st = lambda n: jnp.stack(outs[n])
    kv_p = lambda a: a.reshape(depth, bp, seq, n_heads, d_head)
    kv_s = lambda a: a.reshape(depth, bs, dec_seq, n_heads, d_head)
    return (xp.reshape(bp, seq, d_model), xs.reshape(bs, dec_seq, d_model),
            kv_p(kp_all), kv_p(vp_all), st("hrp"), st("hip"), st("bufp"),
            kv_s(ks_all), kv_s(vs_all), st("hrs"), st("his"), st("bufs"))
```

```python
import functools
import math

import jax
import jax.numpy as jnp
from jax import lax
from jax.experimental import pallas as pl
from jax.experimental.pallas import tpu as pltpu

F32 = jnp.float32
BF16 = jnp.bfloat16
RMS_EPS = 1e-6
LANES = 128
SUBLANES = 8
VMEM_LIMIT = 56 * 1024 * 1024

POOL_WINDOWS = (2, 4, 8, 16)
HALO = 16


def _params(sem, vmem=VMEM_LIMIT):
    return pltpu.CompilerParams(dimension_semantics=sem, vmem_limit_bytes=vmem)


def _rms_scale(x):
    return x * lax.rsqrt(jnp.mean(x * x, axis=-1, keepdims=True) + RMS_EPS)


def _dot(a, b):
    return jnp.dot(a, b, preferred_element_type=F32)


def _ffn_kernel(x_ref, xs_ref, g_ref, wg_ref, wu_ref, wd_ref, gf_ref, o_ref, os_ref, hn_ref, hns_ref,
                *, final_norm):
    i = pl.program_id(0)
    f = pl.program_id(1)
    last = f == pl.num_programs(1) - 1
    wg = wg_ref[...].astype(BF16)
    wu = wu_ref[...].astype(BF16)
    wd = wd_ref[...].astype(BF16)

    def rows(src_ref, dst_ref, h_ref):
        @pl.when(f == 0)
        def _():
            h_ref[...] = (_rms_scale(src_ref[...]) * g_ref[...]).astype(BF16)
            dst_ref[...] = jnp.zeros_like(dst_ref)

        hn = h_ref[...]
        gate = _dot(hn, wg)
        act = (gate * jax.nn.sigmoid(gate) * _dot(hn, wu)).astype(BF16)
        dst_ref[...] += _dot(act, wd)

        @pl.when(last)
        def _():
            y = src_ref[...] + 0.5 * dst_ref[...]
            if final_norm:
                y = _rms_scale(y) * gf_ref[...]
            dst_ref[...] = y

    rows(x_ref, o_ref, hn_ref)
    pl.when(i == 0)(lambda: rows(xs_ref, os_ref, hns_ref))


def _ffn(x, xs, g, wg, wu, wd, g_final, layer, *, final_norm, tm, tf):
    m, d = x.shape
    ms = xs.shape[0]
    d_ff = wg.shape[2]
    assert m % tm == 0 and d_ff % tf == 0
    gain = lambda a: a.reshape(a.shape[:-1] + (1, d))
    return pl.pallas_call(
        functools.partial(_ffn_kernel, final_norm=final_norm),
        grid=(m // tm, d_ff // tf),
        in_specs=[
            pl.BlockSpec((tm, d), lambda i, f: (i, 0), pipeline_mode=pl.Buffered(1)),
            pl.BlockSpec((ms, d), lambda i, f: (0, 0)),
            pl.BlockSpec((None, 1, d), lambda i, f: (layer, 0, 0)),
            pl.BlockSpec((None, d, tf), lambda i, f: (layer, 0, f)),
            pl.BlockSpec((None, d, tf), lambda i, f: (layer, 0, f)),
            pl.BlockSpec((None, tf, d), lambda i, f: (layer, f, 0)),
            pl.BlockSpec((1, d), lambda i, f: (0, 0)),
        ],
        out_specs=[pl.BlockSpec((tm, d), lambda i, f: (i, 0)),
                   pl.BlockSpec((ms, d), lambda i, f: (0, 0))],
        out_shape=[jax.ShapeDtypeStruct((m, d), F32), jax.ShapeDtypeStruct((ms, d), F32)],
        scratch_shapes=[pltpu.VMEM((tm, d), BF16), pltpu.VMEM((ms, d), BF16)],
        compiler_params=_params(("arbitrary", "arbitrary")),
        name="ffn",
    )(x, xs, gain(g), wg, wu, wd, gain(g_final))


def _in_proj_kernel(x_ref, g_ref, w_ref, b_ref, k_all_ref, v_all_ref, q_ref, kb_ref, vb_ref, k_ref,
                    v_ref, us_ref, up_ref, hn_ref, *, na, n_heads):
    del k_all_ref, v_all_ref
    n = pl.program_id(1)
    tm = x_ref.shape[0]
    rc = min(tm, 256)
    hpt = w_ref.shape[1] // LANES

    @pl.when(n == 0)
    def _():
        hn_ref[...] = (_rms_scale(x_ref[...]) * g_ref[...]).astype(BF16)

    def project(store):
        wb = w_ref[...].astype(BF16)
        for r in range(0, tm, rc):
            store(r, _dot(hn_ref[r:r + rc, :], wb) + b_ref[...])

    def to_rows(dst_ref, dtype):
        def store(r, y):
            dst_ref[r:r + rc, :] = y.astype(dtype)
        return store

    def to_heads(b16_ref, rows_ref, c):
        def store(r, y):
            b16_ref[r:r + rc, :] = y.astype(BF16)
            for j in range(hpt):
                rows_ref[pl.ds(r * n_heads + c * hpt + j, rc, stride=n_heads), :] = (
                    y[:, j * LANES:(j + 1) * LANES])
        return store

    pl.when(n < na)(functools.partial(project, to_rows(q_ref, BF16)))
    for c in range(na):
        pl.when(n == na + c)(functools.partial(project, to_heads(kb_ref, k_ref, c)))
        pl.when(n == 2 * na + c)(functools.partial(project, to_heads(vb_ref, v_ref, c)))
    pl.when(n == 3 * na)(functools.partial(project, to_rows(us_ref, F32)))
    pl.when(n == 3 * na + 1)(functools.partial(project, to_rows(up_ref, F32)))


def _in_proj(x, g, w, b, k_all, v_all, layer, *, n_heads, w_ssm, w_pool, tm):
    m, d = x.shape
    w_attn = n_heads * k_all.shape[2]
    tn = w_ssm
    assert w_pool == tn and w_attn % tn == 0 and m % tm == 0 and k_all.shape[1] == m * n_heads
    assert k_all.shape[2] == LANES
    na = w_attn // tn
    n_steps = 3 * na + 2
    assert w.shape[2] == n_steps * tn

    def attn_spec(first):
        return pl.BlockSpec((tm, tn), lambda i, n: (i, jnp.clip(n - first, 0, na - 1)))

    rows_spec = pl.BlockSpec((None, tm * n_heads, LANES), lambda i, n: (layer, i, 0))
    attn_bf16 = jax.ShapeDtypeStruct((m, w_attn), BF16)
    return pl.pallas_call(
        functools.partial(_in_proj_kernel, na=na, n_heads=n_heads),
        grid=(m // tm, n_steps),
        in_specs=[
            pl.BlockSpec((tm, d), lambda i, n: (i, 0), pipeline_mode=pl.Buffered(1)),
            pl.BlockSpec((None, 1, d), lambda i, n: (layer, 0, 0)),
            pl.BlockSpec((None, d, tn), lambda i, n: (layer, 0, n)),
            pl.BlockSpec((None, 1, tn), lambda i, n: (layer, 0, n)),
            pl.BlockSpec(memory_space=pl.ANY),
            pl.BlockSpec(memory_space=pl.ANY),
        ],
        out_specs=[
            attn_spec(0), attn_spec(na), attn_spec(2 * na), rows_spec, rows_spec,
            pl.BlockSpec((tm, tn), lambda i, n: (i, 0)),
            pl.BlockSpec((tm, tn), lambda i, n: (i, 0)),
        ],
        out_shape=[
            attn_bf16, attn_bf16, attn_bf16,
            jax.ShapeDtypeStruct(k_all.shape, F32),
            jax.ShapeDtypeStruct(v_all.shape, F32),
            jax.ShapeDtypeStruct((m, w_ssm), F32),
            jax.ShapeDtypeStruct((m, w_pool), F32),
        ],
        input_output_aliases={4: 3, 5: 4},
        scratch_shapes=[pltpu.VMEM((tm, d), BF16)],
        compiler_params=_params(("parallel", "arbitrary")),
        name="in_proj",
    )(x, g[:, None, :], w, b[:, None, :], k_all, v_all)


def _out_proj_kernel(x_ref, a_ref, s_ref, p_ref, ga_ref, gs_ref, gp_ref, w_ref, o_ref, m_ref,
                     *, w_attn, w_ssm):
    n = pl.program_id(1)

    @pl.when(n == 0)
    def _():
        m_ref[:, 0:w_attn] = (_rms_scale(a_ref[...]) * ga_ref[...]).astype(BF16)
        m_ref[:, w_attn:w_attn + w_ssm] = (_rms_scale(s_ref[...]) * gs_ref[...]).astype(BF16)
        m_ref[:, w_attn + w_ssm:] = (_rms_scale(p_ref[...]) * gp_ref[...]).astype(BF16)

    o_ref[...] = x_ref[...] + _dot(m_ref[...], w_ref[...].astype(BF16))


def _out_proj(x, att, ssm, pool, ga, gs, gp, w, layer, *, tm, tn):
    m, d = x.shape
    wa, ws, wp = att.shape[1], ssm.shape[1], pool.shape[1]
    mix = wa + ws + wp
    assert w.shape[1:] == (mix, d) and m % tm == 0 and d % tn == 0
    return pl.pallas_call(
        functools.partial(_out_proj_kernel, w_attn=wa, w_ssm=ws),
        grid=(m // tm, d // tn),
        in_specs=[
            pl.BlockSpec((tm, tn), lambda i, n: (i, n)),
            pl.BlockSpec((tm, wa), lambda i, n: (i, 0)),
            pl.BlockSpec((tm, ws), lambda i, n: (i, 0)),
            pl.BlockSpec((tm, wp), lambda i, n: (i, 0)),
            pl.BlockSpec((1, wa), lambda i, n: (0, 0)),
            pl.BlockSpec((1, ws), lambda i, n: (0, 0)),
            pl.BlockSpec((1, wp), lambda i, n: (0, 0)),
            pl.BlockSpec((None, mix, tn), lambda i, n: (layer, 0, n)),
        ],
        out_specs=pl.BlockSpec((tm, tn), lambda i, n: (i, n)),
        out_shape=jax.ShapeDtypeStruct((m, d), F32),
        scratch_shapes=[pltpu.VMEM((tm, mix), BF16)],
        compiler_params=_params(("parallel", "arbitrary")),
        name="out_proj",
    )(x, att, ssm, pool, ga.reshape(1, wa), gs.reshape(1, ws), gp.reshape(1, wp), w)


def _sb_terms(z):
    ls = jnp.minimum(z, 0.0) - jnp.log(1.0 + jnp.exp(-jnp.abs(z)))
    return ls, ls - z


def _split_bf16(x):
    hi = x.astype(BF16)
    lo = (x - hi.astype(F32)).astype(BF16)
    return hi, lo


def _sb_log_weights(z, after, mask):
    ls, lk = _sb_terms(z)
    if mask is not None:
        lk = jnp.where(mask, lk, 0.0)
    hi, lo = _split_bf16(lk)
    skip = _dot(hi, after) + _dot(lo, after)
    return ls + skip, skip[:, 0:1] + lk[:, 0:1]


def _sb_weights(log_w, carry, mask):
    w = jnp.exp(log_w + carry)
    if mask is not None:
        w = jnp.where(mask, w, 0.0)
    return w.astype(BF16)


def _after_matrix(n):
    row = lax.broadcasted_iota(jnp.int32, (n, n), 0)
    col = lax.broadcasted_iota(jnp.int32, (n, n), 1)
    return jnp.where(row > col, 1.0, 0.0).astype(BF16), col < row


def _attn_prompt_kernel(q_ref, k_ref, v_ref, o_ref, carry_ref, *, blk, hp, scale):
    qi = pl.program_id(2)
    after, causal = _after_matrix(blk)

    def block(start, mask):
        sls = [slice(hh * LANES, (hh + 1) * LANES) for hh in range(hp)]
        zs = [lax.dot_general(q_ref[:, sl], k_ref[pl.ds(start, blk), sl], (((1,), (1,)), ((), ())),
                              preferred_element_type=F32) * scale for sl in sls]
        lws = [_sb_log_weights(z, after, mask) for z in zs]
        for hh, (sl, (log_w, total)) in enumerate(zip(sls, lws)):
            w = _sb_weights(log_w, carry_ref[hh], mask)
            o_ref[:, sl] += _dot(w, v_ref[pl.ds(start, blk), sl])
            carry_ref[hh] += total

    o_ref[...] = jnp.zeros_like(o_ref)
    carry_ref[...] = jnp.zeros_like(carry_ref)
    block(pl.multiple_of(qi * blk, blk), causal)

    def body(i, c):
        block(pl.multiple_of((qi - 1 - i) * blk, blk), None)
        return c

    lax.fori_loop(0, qi, body, 0)


def _attn_prompt(q, k, v, *, bsz, t_len, n_heads, d_head, blk, hp):
    assert t_len % blk == 0 and d_head == LANES and n_heads % hp == 0
    nq = t_len // blk
    kv_spec = pl.BlockSpec((t_len, hp * d_head), lambda b, h, i: (b, h))
    return pl.pallas_call(
        functools.partial(_attn_prompt_kernel, blk=blk, hp=hp, scale=1.0 / math.sqrt(d_head)),
        grid=(bsz, n_heads // hp, nq),
        in_specs=[pl.BlockSpec((blk, hp * d_head), lambda b, h, i: (b * nq + i, h)), kv_spec, kv_spec],
        out_specs=pl.BlockSpec((blk, hp * d_head), lambda b, h, i: (b * nq + i, h)),
        out_shape=jax.ShapeDtypeStruct(q.shape, F32),
        scratch_shapes=[pltpu.VMEM((hp, blk, 1), F32)],
        compiler_params=_params(("parallel", "parallel", "arbitrary")),
        name="attn_prompt",
    )(q, k, v)


def _attn_sample_kernel(pt_ref, q_ref, kn_ref, vn_ref, *rest, pages_per_step, n_heads, tq, scale):
    del pt_ref
    k_refs = rest[:pages_per_step]
    v_refs = rest[pages_per_step:2 * pages_per_step]
    o_ref, carry_ref = rest[2 * pages_per_step:]
    j = pl.program_id(1)
    page = k_refs[0].shape[0] // n_heads
    after, _ = _after_matrix(page)

    def head_rows(ref, h):
        return ref[pl.ds(h, page, stride=n_heads), :].astype(BF16)

    def log_weights(k_ref, mask):
        zs = [lax.dot_general(q_ref[h], head_rows(k_ref, h), (((1,), (1,)), ((), ())),
                              preferred_element_type=F32) for h in range(n_heads)]
        z = jnp.concatenate(zs, axis=0) * scale
        return _sb_log_weights(z, after, mask)

    def walk(kv_refs, mask):
        lws = [log_weights(k_ref, mask) for k_ref, _ in kv_refs]
        carry = carry_ref[...]
        ws = []
        for log_w, total in lws:
            ws.append(_sb_weights(log_w, carry, mask))
            carry = carry + total
        carry_ref[...] = carry
        for h in range(n_heads):
            rows = slice(h * tq, (h + 1) * tq)
            acc = o_ref[rows, :]
            for w, (_, v_ref) in zip(ws, kv_refs):
                acc = acc + _dot(w[rows, :], head_rows(v_ref, h))
            o_ref[rows, :] = acc

    @pl.when(j == 0)
    def _():
        o_ref[...] = jnp.zeros_like(o_ref)
        carry_ref[...] = jnp.zeros_like(carry_ref)
        key = lax.broadcasted_iota(jnp.int32, (n_heads * tq, page), 1)
        tok = lax.broadcasted_iota(jnp.int32, (n_heads * tq, page), 0) % tq
        walk([(kn_ref, vn_ref)], key < tok)

    walk(list(zip(k_refs, v_refs)), None)


def _attn_sample(q, k_new, v_new, cache_k, cache_v, page_table, layer, *, n_heads, d_head,
                 pages_per_step):
    bs, n_tok, wa = q.shape
    n_pages = page_table.shape[1]
    rows = cache_k.shape[1]
    tq = 2 * SUBLANES
    assert n_pages % pages_per_step == 0 and n_tok <= tq and d_head == LANES
    steps = n_pages // pages_per_step
    q4 = q.reshape(bs, n_tok, n_heads, d_head).transpose(0, 2, 1, 3)
    q4 = jnp.pad(q4, ((0, 0), (0, 0), (0, tq - n_tok), (0, 0)))
    pad = ((0, 0), (0, rows - n_tok * n_heads), (0, 0))
    kn = jnp.pad(k_new, pad)
    vn = jnp.pad(v_new, pad)

    def page_map(i):
        return lambda b, j, pt: (layer + pt[b, n_pages - 1 - (j * pages_per_step + i)], 0, 0)

    page_specs = [pl.BlockSpec((None, rows, d_head), page_map(i)) for i in range(pages_per_step)]
    grid_spec = pltpu.PrefetchScalarGridSpec(
        num_scalar_prefetch=1,
        grid=(bs, steps),
        in_specs=[
            pl.BlockSpec((None, n_heads, tq, d_head), lambda b, j, pt: (b, 0, 0, 0)),
            pl.BlockSpec((None, rows, d_head), lambda b, j, pt: (b, 0, 0)),
            pl.BlockSpec((None, rows, d_head), lambda b, j, pt: (b, 0, 0)),
        ] + page_specs + page_specs,
        out_specs=pl.BlockSpec((None, n_heads * tq, d_head), lambda b, j, pt: (b, 0, 0)),
        scratch_shapes=[pltpu.VMEM((n_heads * tq, 1), F32)],
    )
    out = pl.pallas_call(
        functools.partial(_attn_sample_kernel, pages_per_step=pages_per_step, n_heads=n_heads,
                          tq=tq, scale=1.0 / math.sqrt(d_head)),
        grid_spec=grid_spec,
        out_shape=jax.ShapeDtypeStruct((bs, n_heads * tq, d_head), F32),
        compiler_params=_params(("parallel", "arbitrary")),
        name="attn_sample",
    )(page_table, q4, kn, vn, *([cache_k] * pages_per_step), *([cache_v] * pages_per_step))
    out = out.reshape(bs, n_heads, tq, d_head)[:, :, :n_tok]
    return out.transpose(0, 2, 1, 3).reshape(bs, n_tok, wa)


def _pool_kernel(halo_ref, u_ref, w_ref, sc_ref, o_ref, ext_ref, *, tp, pos0):
    t = pl.program_id(1)

    @pl.when(t == 0)
    def _():
        ext_ref[0:HALO, :] = halo_ref[...]

    ext_ref[HALO:HALO + tp, :] = u_ref[...]
    pos = pos0 + t * tp + lax.broadcasted_iota(jnp.int32, (tp, 1), 0)
    for g, win in enumerate(POOL_WINDOWS):
        sl = slice(g * LANES, (g + 1) * LANES)
        s = ext_ref[HALO:HALO + tp, sl]
        for i in range(1, win):
            s = s + ext_ref[HALO - i:HALO - i + tp, sl]
        cnt = jnp.minimum(pos + 1, win).astype(F32)
        diff = s / cnt - u_ref[:, sl]
        o_ref[:, sl] = _dot(diff.astype(BF16), w_ref[g].astype(BF16)) * sc_ref[:, sl]
    ext_ref[0:HALO, :] = ext_ref[tp:tp + HALO, :]


def _pool(u, halo, w, scale, *, tp, pos0):
    bsz, t_len, wp = u.shape
    assert t_len % tp == 0 and wp == len(POOL_WINDOWS) * LANES
    return pl.pallas_call(
        functools.partial(_pool_kernel, tp=tp, pos0=pos0),
        grid=(bsz, t_len // tp),
        in_specs=[
            pl.BlockSpec((None, HALO, wp), lambda b, t: (b, 0, 0)),
            pl.BlockSpec((None, tp, wp), lambda b, t: (b, t, 0)),
            pl.BlockSpec(w.shape, lambda b, t: (0, 0, 0)),
            pl.BlockSpec((1, wp), lambda b, t: (0, 0)),
        ],
        out_specs=pl.BlockSpec((None, tp, wp), lambda b, t: (b, t, 0)),
        out_shape=jax.ShapeDtypeStruct(u.shape, F32),
        scratch_shapes=[pltpu.VMEM((HALO + tp, wp), F32)],
        compiler_params=_params(("parallel", "arbitrary")),
        name="pool",
    )(halo, u, w, scale.reshape(1, wp))


def _ssm_prep_kernel(lr_ref, li_ref, ldt_ref, br_ref, bi_ref, lbr_ref, lbi_ref, bbr_ref, bbi_ref):
    lr = lr_ref[...]
    li = li_ref[...]
    dt = jnp.exp(ldt_ref[...])
    mag = jnp.exp(lr * dt)
    ang = li * dt
    lbr = mag * jnp.cos(ang)
    lbi = mag * jnp.sin(ang)
    nr = lbr - 1.0
    den = lr * lr + li * li
    fr = (nr * lr + lbi * li) / den
    fi = (lbi * lr - nr * li) / den
    br = br_ref[...]
    bi = bi_ref[...]
    lbr_ref[...] = lbr
    lbi_ref[...] = lbi
    bbr_ref[...] = fr * br - fi * bi
    bbi_ref[...] = fr * bi + fi * br


def _ssm_prep(lam_re, lam_im, log_dt, b_re, b_im):
    depth, g, p = lam_re.shape
    ch = b_re.shape[-1]
    shape = (depth, g, ch, p)
    bc = lambda a: jnp.broadcast_to(a[:, :, None, :], shape)
    spec = pl.BlockSpec((None, g, ch, p), lambda l: (l, 0, 0, 0))
    out = jax.ShapeDtypeStruct(shape, F32)
    lbr, lbi, bbr, bbi = pl.pallas_call(
        _ssm_prep_kernel,
        grid=(depth,),
        in_specs=[spec] * 5,
        out_specs=[spec] * 4,
        out_shape=[out] * 4,
        compiler_params=_params(("parallel",)),
        name="ssm_prep",
    )(bc(lam_re), bc(lam_im), jnp.broadcast_to(log_dt[:, :, None, None], shape),
      b_re.transpose(0, 1, 3, 2), b_im.transpose(0, 1, 3, 2))
    return lbr[:, :, 0, :], lbi[:, :, 0, :], bbr, bbi


def _gelu_glu(y, wglu_ref):
    g = jax.nn.gelu(y, approximate=True)
    return g * jax.nn.sigmoid(_dot(g.astype(BF16), wglu_ref[...].astype(BF16)))


def _ssm_prompt_kernel(u_ref, bh_ref, cre_ref, cim_ref, lre_ref, lim_ref, d_ref, wglu_ref,
                       y_ref, hre_ref, him_ref, buf_ref, *, lc, nb, half_in, half_out):
    c = pl.program_id(0)
    nslab = buf_ref.shape[0] // 2
    rows = 2 * nb

    @pl.when(c == 0)
    def _():
        hre_ref[...] = jnp.zeros_like(hre_ref)
        him_ref[...] = jnp.zeros_like(him_ref)

    for half in range(2):
        for b in range(nb):
            s = half * nb + b
            ub = u_ref[b, :, half * half_in:(half + 1) * half_in].astype(BF16)
            bu = _dot(ub, bh_ref[half])
            for j in range(2 * nslab):
                buf_ref[j, pl.ds(s, lc, stride=rows), :] = bu[:, j * LANES:(j + 1) * LANES]

    lam_r = [lre_ref[:, j * LANES:(j + 1) * LANES] for j in range(nslab)]
    lam_i = [lim_ref[:, j * LANES:(j + 1) * LANES] for j in range(nslab)]
    init = (tuple(hre_ref[:, j * LANES:(j + 1) * LANES] for j in range(nslab)),
            tuple(him_ref[:, j * LANES:(j + 1) * LANES] for j in range(nslab)))

    def step(k, carry):
        hr, hi = carry
        r0 = pl.multiple_of(k * rows, rows)
        new_r, new_i = [], []
        for j in range(nslab):
            xr = buf_ref[j, pl.ds(r0, rows), :]
            xi = buf_ref[nslab + j, pl.ds(r0, rows), :]
            nr = lam_r[j] * hr[j] - lam_i[j] * hi[j] + xr
            ni = lam_r[j] * hi[j] + lam_i[j] * hr[j] + xi
            buf_ref[j, pl.ds(r0, rows), :] = nr
            buf_ref[nslab + j, pl.ds(r0, rows), :] = ni
            new_r.append(nr)
            new_i.append(ni)
        return tuple(new_r), tuple(new_i)

    hr, hi = lax.fori_loop(0, lc, step, init, unroll=2)
    for j in range(nslab):
        hre_ref[:, j * LANES:(j + 1) * LANES] = hr[j]
        him_ref[:, j * LANES:(j + 1) * LANES] = hi[j]

    for b in range(nb):
        parts = []
        for half in range(2):
            s = half * nb + b
            h_re = jnp.concatenate(
                [buf_ref[j, pl.ds(s, lc, stride=rows), :] for j in range(nslab)], axis=1)
            h_im = jnp.concatenate(
                [buf_ref[nslab + j, pl.ds(s, lc, stride=rows), :] for j in range(nslab)], axis=1)
            parts.append(_dot(h_re.astype(BF16), cre_ref[half]) - _dot(h_im.astype(BF16), cim_ref[half]))
        y = jnp.concatenate(parts, axis=1) + d_ref[...] * u_ref[b]
        y_ref[b] = _gelu_glu(y, wglu_ref)


def _ssm_prompt(u, bh, cre, cim, lam_re8, lam_im8, d_skip, w_glu, *, lc):
    nb, t_len, w = u.shape
    s2 = lam_re8.shape[1]
    assert 2 * nb == SUBLANES and t_len % lc == 0 and s2 % LANES == 0
    nslab = s2 // LANES
    st = jax.ShapeDtypeStruct((2 * nb, s2), F32)
    full = lambda a: pl.BlockSpec(a.shape, lambda c: (0,) * a.ndim)
    d2 = d_skip.reshape(1, w)
    return pl.pallas_call(
        functools.partial(_ssm_prompt_kernel, lc=lc, nb=nb, half_in=w // 2, half_out=w // 2),
        grid=(t_len // lc,),
        in_specs=[pl.BlockSpec((nb, lc, w), lambda c: (0, c, 0)),
                  full(bh), full(cre), full(cim), full(lam_re8), full(lam_im8), full(d2), full(w_glu)],
        out_specs=[pl.BlockSpec((nb, lc, w), lambda c: (0, c, 0)),
                   pl.BlockSpec((2 * nb, s2), lambda c: (0, 0)),
                   pl.BlockSpec((2 * nb, s2), lambda c: (0, 0))],
        out_shape=[jax.ShapeDtypeStruct(u.shape, F32), st, st],
        scratch_shapes=[pltpu.VMEM((2 * nslab, lc * 2 * nb, LANES), F32)],
        compiler_params=_params(("arbitrary",)),
        name="ssm_prompt",
    )(u, bh, cre, cim, lam_re8, lam_im8, d2, w_glu)


def _ssm_sample_kernel(u_ref, h0r_ref, h0i_ref, bre_ref, bim_ref, cre_ref, cim_ref, lre_ref, lim_ref,
                       d_ref, wglu_ref, y_ref, hre_ref, him_ref, *, n_tok):
    hr = h0r_ref[...]
    hi = h0i_ref[...]
    lr = lre_ref[...]
    li = lim_ref[...]
    for t in range(n_tok):
        u = u_ref[t]
        ub = u.astype(BF16)
        hr, hi = (lr * hr - li * hi + _dot(ub, bre_ref[...]),
                  lr * hi + li * hr + _dot(ub, bim_ref[...]))
        y = (_dot(hr.astype(BF16), cre_ref[...]) - _dot(hi.astype(BF16), cim_ref[...])
             + d_ref[...] * u)
        y_ref[t] = _gelu_glu(y, wglu_ref)
    hre_ref[...] = hr
    him_ref[...] = hi


def _ssm_sample(u_t, h0_re, h0_im, bbd_re, bbd_im, cbd_re, cbd_im, lam_re, lam_im, d_skip, w_glu):
    n_tok, bs, w = u_t.shape
    s = h0_re.shape[1]
    st = jax.ShapeDtypeStruct((bs, s), F32)
    lam_b = lambda a: jnp.broadcast_to(a.reshape(1, s), (bs, s))
    return pl.pallas_call(
        functools.partial(_ssm_sample_kernel, n_tok=n_tok),
        out_shape=[jax.ShapeDtypeStruct(u_t.shape, F32), st, st],
        compiler_params=_params(None),
        name="ssm_sample",
    )(u_t, h0_re, h0_im, bbd_re, bbd_im, cbd_re, cbd_im, lam_b(lam_re), lam_b(lam_im),
      d_skip.reshape(1, w), w_glu)


def _block_diag_b(bb):
    depth, g, ch, p = bb.shape
    return jnp.einsum("lgcp,gh->lgchp", bb, jnp.eye(g, dtype=bb.dtype)).reshape(depth, g * ch, g * p)


def _block_diag_c(c):
    depth, g, ch, p = c.shape
    return jnp.einsum("lghp,gk->lgpkh", c, jnp.eye(g, dtype=c.dtype)).reshape(depth, g * p, g * ch)


def kernel(x_prompt, x_sample, cache_k, cache_v, state_ssm_re, state_ssm_im, state_pool, page_table,
           norm_ffn1, ffn1_w_gate, ffn1_w_up, ffn1_w_down, norm_mix, w_in, b_in, ssm_lambda_re,
           ssm_lambda_im, ssm_log_dt, ssm_b_re, ssm_b_im, ssm_c_re, ssm_c_im, ssm_d, ssm_w_glu,
           pool_w, pool_scale, norm_attn_out, norm_ssm_out, norm_pool_out, w_out, norm_ffn2,
           ffn2_w_gate, ffn2_w_up, ffn2_w_down, norm_final):
    bp, seq, d_model = x_prompt.shape
    bs, dec_seq, _ = x_sample.shape
    depth, n_pool, page, n_heads, d_head = cache_k.shape
    w_attn = n_heads * d_head
    _, n_groups, n_state = ssm_lambda_re.shape
    ssm_ch = ssm_b_re.shape[-1]
    w_ssm = n_groups * ssm_ch
    w_pool = pool_scale.shape[-1]
    pool_buf = state_pool.shape[2]
    past_len = page_table.shape[1] * page
    mp, ms = bp * seq, bs * dec_seq
    s_all = n_groups * n_state
    s_half = s_all // 2

    lb_re, lb_im, bb_re, bb_im = _ssm_prep(ssm_lambda_re, ssm_lambda_im, ssm_log_dt, ssm_b_re, ssm_b_im)
    bbd_re = _block_diag_b(bb_re).astype(BF16)
    bbd_im = _block_diag_b(bb_im).astype(BF16)
    cbd_re = _block_diag_c(ssm_c_re).astype(BF16)
    cbd_im = _block_diag_c(ssm_c_im).astype(BF16)
    lam_re = lb_re.reshape(depth, 1, s_all)
    lam_im = lb_im.reshape(depth, 1, s_all)
    hw = w_ssm // 2

    def halves_b(l):
        return jnp.stack([jnp.concatenate([bbd_re[l, h * hw:(h + 1) * hw, h * s_half:(h + 1) * s_half],
                                           bbd_im[l, h * hw:(h + 1) * hw, h * s_half:(h + 1) * s_half]],
                                          axis=1) for h in range(2)])

    def halves_c(cbd, l):
        return jnp.stack([cbd[l, h * s_half:(h + 1) * s_half, h * hw:(h + 1) * hw] for h in range(2)])

    def lam8(lam, l):
        return jnp.repeat(lam[l].reshape(2, s_half), bp, axis=0)

    cache_k2 = cache_k.reshape(depth * n_pool, page * n_heads, d_head)
    cache_v2 = cache_v.reshape(depth * n_pool, page * n_heads, d_head)
    zero_halo = jnp.zeros((bp, HALO, w_pool), F32)
    pad_t = SUBLANES - dec_seq

    xp = x_prompt.reshape(mp, d_model)
    xs = x_sample.reshape(ms, d_model)
    kp_all = jnp.zeros((depth, mp * n_heads, d_head), F32)
    vp_all = jnp.zeros((depth, mp * n_heads, d_head), F32)
    ks_all = jnp.zeros((depth, ms * n_heads, d_head), F32)
    vs_all = jnp.zeros((depth, ms * n_heads, d_head), F32)
    outs = {n: [] for n in ("hrp", "hip", "bufp", "hrs", "his", "bufs")}
    tm_p = 1024
    for l in range(depth):
        xp, xs = _ffn(xp, xs, norm_ffn1, ffn1_w_gate, ffn1_w_up, ffn1_w_down, norm_final, l,
                      final_norm=False, tm=tm_p, tf=256)
        q, kb, vb, kp_all, vp_all, us, up = _in_proj(
            xp, norm_mix, w_in, b_in, kp_all, vp_all, l, n_heads=n_heads, w_ssm=w_ssm,
            w_pool=w_pool, tm=tm_p)
        att = _attn_prompt(q, kb, vb, bsz=bp, t_len=seq, n_heads=n_heads, d_head=d_head, blk=256,
                           hp=4)
        ssm, st_re, st_im = _ssm_prompt(us.reshape(bp, seq, w_ssm), halves_b(l), halves_c(cbd_re, l),
                                        halves_c(cbd_im, l), lam8(lam_re, l), lam8(lam_im, l),
                                        ssm_d[l], ssm_w_glu[l], lc=128)
        up3 = up.reshape(bp, seq, w_pool)
        pool = _pool(up3, zero_halo, pool_w[l], pool_scale[l], tp=512, pos0=0)
        xp = _out_proj(xp, att, ssm.reshape(mp, w_ssm), pool.reshape(mp, w_pool), norm_attn_out[l],
                       norm_ssm_out[l], norm_pool_out[l], w_out, l, tm=tm_p, tn=512)
        unhalf = lambda a: a.reshape(2, bp, n_groups // 2, n_state).transpose(1, 0, 2, 3).reshape(
            bp, n_groups, n_state)
        outs["hrp"].append(unhalf(st_re))
        outs["hip"].append(unhalf(st_im))
        outs["bufp"].append(up3[:, seq - pool_buf:, :])

        q, _, _, ks_all, vs_all, us, up = _in_proj(
            xs, norm_mix, w_in, b_in, ks_all, vs_all, l, n_heads=n_heads, w_ssm=w_ssm,
            w_pool=w_pool, tm=ms)
        att = _attn_sample(q.reshape(bs, dec_seq, w_attn),
                           ks_all[l].reshape(bs, dec_seq * n_heads, d_head),
                           vs_all[l].reshape(bs, dec_seq * n_heads, d_head), cache_k2, cache_v2,
                           page_table, l * n_pool, n_heads=n_heads, d_head=d_head, pages_per_step=8)
        u_t = us.reshape(bs, dec_seq, w_ssm).transpose(1, 0, 2)
        ssm_t, st_re, st_im = _ssm_sample(u_t, state_ssm_re[l].reshape(bs, s_all),
                                          state_ssm_im[l].reshape(bs, s_all), bbd_re[l], bbd_im[l],
                                          cbd_re[l], cbd_im[l], lam_re[l], lam_im[l], ssm_d[l],
                                          ssm_w_glu[l])
        ssm = ssm_t.transpose(1, 0, 2).reshape(ms, w_ssm)
        up3 = up.reshape(bs, dec_seq, w_pool)
        halo = jnp.pad(state_pool[l], ((0, 0), (HALO - pool_buf, 0), (0, 0)))
        pool = _pool(jnp.pad(up3, ((0, 0), (0, pad_t), (0, 0))), halo, pool_w[l], pool_scale[l],
                     tp=SUBLANES, pos0=past_len)[:, :dec_seq]
        xs = _out_proj(xs, att.reshape(ms, w_attn), ssm, pool.reshape(ms, w_pool), norm_attn_out[l],
                       norm_ssm_out[l], norm_pool_out[l], w_out, l, tm=ms, tn=512)
        xp, xs = _ffn(xp, xs, norm_ffn2, ffn2_w_gate, ffn2_w_up, ffn2_w_down, norm_final, l,
                      final_norm=(l == depth - 1), tm=tm_p, tf=256)
        outs["hrs"].append(st_re.reshape(bs, n_groups, n_state))
        outs["his"].append(st_im.reshape(bs, n_groups, n_state))
        outs["bufs"].append(jnp.concatenate([state_pool[l], up3], axis=1)[:, -pool_buf:])

    st = lambda n: jnp.stack(outs[n])
    kv_p = lambda a: a.reshape(depth, bp, seq, n_heads, d_head)
    kv_s = lambda a: a.reshape(depth, bs, dec_seq, n_heads, d_head)
    return (xp.reshape(bp, seq, d_model), xs.reshape(bs, dec_seq, d_model),
            kv_p(kp_all), kv_p(vp_all), st("hrp"), st("hip"), st("bufp"),
            kv_s(ks_all), kv_s(vs_all), st("hrs"), st("his"), st("bufs"))
```

```python
import functools
import math

import jax
import jax.numpy as jnp
from jax import lax
from jax.experimental import pallas as pl
from jax.experimental.pallas import tpu as pltpu

F32 = jnp.float32
BF16 = jnp.bfloat16
RMS_EPS = 1e-6
LANES = 128
SUBLANES = 8
VMEM_LIMIT = 56 * 1024 * 1024

POOL_WINDOWS = (2, 4, 8, 16)
HALO = 16


def _params(sem, vmem=VMEM_LIMIT):
    return pltpu.CompilerParams(dimension_semantics=sem, vmem_limit_bytes=vmem)


def _rms_scale(x):
    return x * lax.rsqrt(jnp.mean(x * x, axis=-1, keepdims=True) + RMS_EPS)


def _dot(a, b):
    return jnp.dot(a, b, preferred_element_type=F32)


def _ffn_kernel(x_ref, xs_ref, g_ref, wg_ref, wu_ref, wd_ref, o_ref, os_ref, hn_ref, hns_ref):
    i = pl.program_id(0)
    f = pl.program_id(1)
    last = f == pl.num_programs(1) - 1
    tf = wg_ref.shape[1]
    fc = min(tf, 256)

    def rows(src_ref, dst_ref, h_ref):
        n_rows = src_ref.shape[0]
        rc = min(n_rows, 256)

        @pl.when(f == 0)
        def _():
            for r in range(0, n_rows, rc):
                h_ref[r:r + rc, :] = (_rms_scale(src_ref[r:r + rc, :]) * g_ref[...]).astype(BF16)
            dst_ref[...] = jnp.zeros_like(dst_ref)

        hn = h_ref[...]
        for c in range(0, tf, fc):
            gate = _dot(hn, wg_ref[:, c:c + fc].astype(BF16))
            up = _dot(hn, wu_ref[:, c:c + fc].astype(BF16))
            act = (gate * jax.nn.sigmoid(gate) * up).astype(BF16)
            dst_ref[...] += _dot(act, wd_ref[c:c + fc, :].astype(BF16))

        @pl.when(last)
        def _():
            for r in range(0, n_rows, rc):
                dst_ref[r:r + rc, :] = src_ref[r:r + rc, :] + 0.5 * dst_ref[r:r + rc, :]

    rows(x_ref, o_ref, hn_ref)
    pl.when(i == 0)(lambda: rows(xs_ref, os_ref, hns_ref))


def _ffn(x, xs, g, wg, wu, wd, layer, *, tm, tf):
    m, d = x.shape
    ms = xs.shape[0]
    depth, _, d_ff = wg.shape
    assert m % tm == 0 and d_ff % tf == 0
    nf = d_ff // tf
    once = pl.Buffered(1)
    return pl.pallas_call(
        _ffn_kernel,
        grid=(m // tm, nf),
        in_specs=[
            pl.BlockSpec((tm, d), lambda i, f: (i, 0), pipeline_mode=once),
            pl.BlockSpec((ms, d), lambda i, f: (0, 0)),
            pl.BlockSpec((None, 1, d), lambda i, f: (layer, 0, 0)),
            pl.BlockSpec((d, tf), lambda i, f: (layer, f)),
            pl.BlockSpec((d, tf), lambda i, f: (layer, f)),
            pl.BlockSpec((tf, d), lambda i, f: (layer * nf + f, 0)),
        ],
        out_specs=[pl.BlockSpec((tm, d), lambda i, f: (i, 0), pipeline_mode=once),
                   pl.BlockSpec((ms, d), lambda i, f: (0, 0))],
        out_shape=[jax.ShapeDtypeStruct((m, d), F32), jax.ShapeDtypeStruct((ms, d), F32)],
        scratch_shapes=[pltpu.VMEM((tm, d), BF16), pltpu.VMEM((ms, d), BF16)],
        compiler_params=_params(("arbitrary", "arbitrary")),
        name="ffn",
    )(x, xs, g[:, None, :], wg.reshape(depth * d, d_ff), wu.reshape(depth * d, d_ff),
      wd.reshape(depth * d_ff, d))


def _rmsnorm_kernel(x_ref, g_ref, o_ref):
    o_ref[...] = _rms_scale(x_ref[...]) * g_ref[...]


def _rmsnorm(x, g, *, tm):
    m, d = x.shape
    assert m % tm == 0
    return pl.pallas_call(
        _rmsnorm_kernel,
        grid=(m // tm,),
        in_specs=[pl.BlockSpec((tm, d), lambda i: (i, 0)), pl.BlockSpec((1, d), lambda i: (0, 0))],
        out_specs=pl.BlockSpec((tm, d), lambda i: (i, 0)),
        out_shape=jax.ShapeDtypeStruct((m, d), F32),
        compiler_params=_params(("parallel",)),
        name="final_norm",
    )(x, g.reshape(1, d))


def _in_proj_kernel(x_ref, g_ref, w_ref, b_ref, k_all_ref, v_all_ref, q_ref, kb_ref, vb_ref, k_ref,
                    v_ref, us_ref, up_ref, hn_ref, *, na, n_heads):
    del k_all_ref, v_all_ref
    n = pl.program_id(1)
    tm = x_ref.shape[0]
    rc = min(tm, 256)
    hpt = w_ref.shape[1] // LANES

    @pl.when(n == 0)
    def _():
        hn_ref[...] = (_rms_scale(x_ref[...]) * g_ref[...]).astype(BF16)

    def project(store):
        wb = w_ref[...].astype(BF16)
        for r in range(0, tm, rc):
            store(r, _dot(hn_ref[r:r + rc, :], wb) + b_ref[...])

    def to_rows(dst_ref, dtype):
        def store(r, y):
            dst_ref[r:r + rc, :] = y.astype(dtype)
        return store

    def to_heads(b16_ref, rows_ref, c):
        def store(r, y):
            b16_ref[r:r + rc, :] = y.astype(BF16)
            for j in range(hpt):
                rows_ref[pl.ds(r * n_heads + c * hpt + j, rc, stride=n_heads), :] = (
                    y[:, j * LANES:(j + 1) * LANES])
        return store

    pl.when(n < na)(functools.partial(project, to_rows(q_ref, BF16)))
    for c in range(na):
        pl.when(n == na + c)(functools.partial(project, to_heads(kb_ref, k_ref, c)))
        pl.when(n == 2 * na + c)(functools.partial(project, to_heads(vb_ref, v_ref, c)))
    pl.when(n == 3 * na)(functools.partial(project, to_rows(us_ref, F32)))
    pl.when(n == 3 * na + 1)(functools.partial(project, to_rows(up_ref, F32)))


def _in_proj(x, g, w, b, k_all, v_all, layer, *, n_heads, w_ssm, w_pool, tm):
    m, d = x.shape
    w_attn = n_heads * k_all.shape[2]
    tn = w_ssm
    assert w_pool == tn and w_attn % tn == 0 and m % tm == 0 and k_all.shape[1] == m * n_heads
    assert k_all.shape[2] == LANES
    na = w_attn // tn
    n_steps = 3 * na + 2
    assert w.shape[2] == n_steps * tn

    def attn_spec(first):
        return pl.BlockSpec((tm, tn), lambda i, n: (i, jnp.clip(n - first, 0, na - 1)))

    rows_spec = pl.BlockSpec((None, tm * n_heads, LANES), lambda i, n: (layer, i, 0))
    attn_bf16 = jax.ShapeDtypeStruct((m, w_attn), BF16)
    return pl.pallas_call(
        functools.partial(_in_proj_kernel, na=na, n_heads=n_heads),
        grid=(m // tm, n_steps),
        in_specs=[
            pl.BlockSpec((tm, d), lambda i, n: (i, 0), pipeline_mode=pl.Buffered(1)),
            pl.BlockSpec((None, 1, d), lambda i, n: (layer, 0, 0)),
            pl.BlockSpec((d, tn), lambda i, n: (layer, n)),
            pl.BlockSpec((None, 1, tn), lambda i, n: (layer, 0, n)),
            pl.BlockSpec(memory_space=pl.ANY),
            pl.BlockSpec(memory_space=pl.ANY),
        ],
        out_specs=[
            attn_spec(0), attn_spec(na), attn_spec(2 * na), rows_spec, rows_spec,
            pl.BlockSpec((tm, tn), lambda i, n: (i, 0)),
            pl.BlockSpec((tm, tn), lambda i, n: (i, 0)),
        ],
        out_shape=[
            attn_bf16, attn_bf16, attn_bf16,
            jax.ShapeDtypeStruct(k_all.shape, F32),
            jax.ShapeDtypeStruct(v_all.shape, F32),
            jax.ShapeDtypeStruct((m, w_ssm), F32),
            jax.ShapeDtypeStruct((m, w_pool), F32),
        ],
        input_output_aliases={4: 3, 5: 4},
        scratch_shapes=[pltpu.VMEM((tm, d), BF16)],
        compiler_params=_params(("parallel", "arbitrary")),
        name="in_proj",
    )(x, g[:, None, :], w.reshape(-1, w.shape[2]), b[:, None, :], k_all, v_all)


def _out_proj_kernel(x_ref, a_ref, s_ref, p_ref, ga_ref, gs_ref, gp_ref, w_ref, o_ref, m_ref,
                     *, w_attn, w_ssm):
    n = pl.program_id(1)

    @pl.when(n == 0)
    def _():
        m_ref[:, 0:w_attn] = (_rms_scale(a_ref[...]) * ga_ref[...]).astype(BF16)
        m_ref[:, w_attn:w_attn + w_ssm] = (_rms_scale(s_ref[...]) * gs_ref[...]).astype(BF16)
        m_ref[:, w_attn + w_ssm:] = (_rms_scale(p_ref[...]) * gp_ref[...]).astype(BF16)

    o_ref[...] = x_ref[...] + _dot(m_ref[...], w_ref[...].astype(BF16))


def _out_proj(x, att, ssm, pool, ga, gs, gp, w, layer, *, tm, tn):
    m, d = x.shape
    wa, ws, wp = att.shape[1], ssm.shape[1], pool.shape[1]
    mix = wa + ws + wp
    assert w.shape[1:] == (mix, d) and m % tm == 0 and d % tn == 0
    return pl.pallas_call(
        functools.partial(_out_proj_kernel, w_attn=wa, w_ssm=ws),
        grid=(m // tm, d // tn),
        in_specs=[
            pl.BlockSpec((tm, tn), lambda i, n: (i, n)),
            pl.BlockSpec((tm, wa), lambda i, n: (i, 0)),
            pl.BlockSpec((tm, ws), lambda i, n: (i, 0)),
            pl.BlockSpec((tm, wp), lambda i, n: (i, 0)),
            pl.BlockSpec((1, wa), lambda i, n: (0, 0)),
            pl.BlockSpec((1, ws), lambda i, n: (0, 0)),
            pl.BlockSpec((1, wp), lambda i, n: (0, 0)),
            pl.BlockSpec((mix, tn), lambda i, n: (layer, n)),
        ],
        out_specs=pl.BlockSpec((tm, tn), lambda i, n: (i, n)),
        out_shape=jax.ShapeDtypeStruct((m, d), F32),
        scratch_shapes=[pltpu.VMEM((tm, mix), BF16)],
        compiler_params=_params(("parallel", "arbitrary")),
        name="out_proj",
    )(x, att, ssm, pool, ga.reshape(1, wa), gs.reshape(1, ws), gp.reshape(1, wp), w.reshape(-1, d))


def _sb_terms(z):
    ls = jnp.minimum(z, 0.0) - jnp.log(1.0 + jnp.exp(-jnp.abs(z)))
    return ls, ls - z


def _split_bf16(x):
    hi = x.astype(BF16)
    lo = (x - hi.astype(F32)).astype(BF16)
    return hi, lo


def _sb_log_weights(z, after, mask):
    ls, lk = _sb_terms(z)
    if mask is not None:
        lk = jnp.where(mask, lk, 0.0)
    hi, lo = _split_bf16(lk)
    skip = _dot(hi, after) + _dot(lo, after)
    return ls + skip, skip[:, 0:1] + lk[:, 0:1]


def _sb_weights(log_w, carry, mask):
    w = jnp.exp(log_w + carry)
    if mask is not None:
        w = jnp.where(mask, w, 0.0)
    return w.astype(BF16)


def _after_matrix(n):
    row = lax.broadcasted_iota(jnp.int32, (n, n), 0)
    col = lax.broadcasted_iota(jnp.int32, (n, n), 1)
    return jnp.where(row > col, 1.0, 0.0).astype(BF16), col < row


def _attn_prompt_kernel(q_ref, k_ref, v_ref, o_ref, carry_ref, *, blk, hp, scale):
    qi = pl.program_id(2)
    after, causal = _after_matrix(blk)

    def block(start, mask):
        sls = [slice(hh * LANES, (hh + 1) * LANES) for hh in range(hp)]
        zs = [lax.dot_general(q_ref[:, sl], k_ref[pl.ds(start, blk), sl], (((1,), (1,)), ((), ())),
                              preferred_element_type=F32) * scale for sl in sls]
        lws = [_sb_log_weights(z, after, mask) for z in zs]
        for hh, (sl, (log_w, total)) in enumerate(zip(sls, lws)):
            w = _sb_weights(log_w, carry_ref[hh], mask)
            o_ref[:, sl] += _dot(w, v_ref[pl.ds(start, blk), sl])
            carry_ref[hh] += total

    o_ref[...] = jnp.zeros_like(o_ref)
    carry_ref[...] = jnp.zeros_like(carry_ref)
    block(pl.multiple_of(qi * blk, blk), causal)

    def body(i, c):
        block(pl.multiple_of((qi - 1 - i) * blk, blk), None)
        return c

    lax.fori_loop(0, qi, body, 0)


def _attn_prompt(q, k, v, *, bsz, t_len, n_heads, d_head, blk, hp):
    assert t_len % blk == 0 and d_head == LANES and n_heads % hp == 0
    nq = t_len // blk
    kv_spec = pl.BlockSpec((t_len, hp * d_head), lambda b, h, i: (b, h))
    return pl.pallas_call(
        functools.partial(_attn_prompt_kernel, blk=blk, hp=hp, scale=1.0 / math.sqrt(d_head)),
        grid=(bsz, n_heads // hp, nq),
        in_specs=[pl.BlockSpec((blk, hp * d_head), lambda b, h, i: (b * nq + i, h)), kv_spec, kv_spec],
        out_specs=pl.BlockSpec((blk, hp * d_head), lambda b, h, i: (b * nq + i, h)),
        out_shape=jax.ShapeDtypeStruct(q.shape, F32),
        scratch_shapes=[pltpu.VMEM((hp, blk, 1), F32)],
        compiler_params=_params(("parallel", "parallel", "arbitrary")),
        name="attn_prompt",
    )(q, k, v)


def _attn_sample_kernel(pt_ref, q_ref, kn_ref, vn_ref, *rest, pages_per_step, n_heads, tq, scale):
    del pt_ref
    k_refs = rest[:pages_per_step]
    v_refs = rest[pages_per_step:2 * pages_per_step]
    o_ref, carry_ref = rest[2 * pages_per_step:]
    j = pl.program_id(1)
    page = k_refs[0].shape[0] // n_heads
    after, _ = _after_matrix(page)

    def head_rows(ref, h):
        return ref[pl.ds(h, page, stride=n_heads), :].astype(BF16)

    def log_weights(k_ref, mask):
        zs = [lax.dot_general(q_ref[h], head_rows(k_ref, h), (((1,), (1,)), ((), ())),
                              preferred_element_type=F32) for h in range(n_heads)]
        z = jnp.concatenate(zs, axis=0) * scale
        return _sb_log_weights(z, after, mask)

    def walk(kv_refs, mask):
        lws = [log_weights(k_ref, mask) for k_ref, _ in kv_refs]
        carry = carry_ref[...]
        ws = []
        for log_w, total in lws:
            ws.append(_sb_weights(log_w, carry, mask))
            carry = carry + total
        carry_ref[...] = carry
        for h in range(n_heads):
            rows = slice(h * tq, (h + 1) * tq)
            acc = o_ref[rows, :]
            for w, (_, v_ref) in zip(ws, kv_refs):
                acc = acc + _dot(w[rows, :], head_rows(v_ref, h))
            o_ref[rows, :] = acc

    @pl.when(j == 0)
    def _():
        o_ref[...] = jnp.zeros_like(o_ref)
        carry_ref[...] = jnp.zeros_like(carry_ref)
        key = lax.broadcasted_iota(jnp.int32, (n_heads * tq, page), 1)
        tok = lax.broadcasted_iota(jnp.int32, (n_heads * tq, page), 0) % tq
        walk([(kn_ref, vn_ref)], key < tok)

    walk(list(zip(k_refs, v_refs)), None)


def _attn_sample(q, k_new, v_new, cache_k, cache_v, page_table, layer, *, n_heads, d_head,
                 pages_per_step):
    bs, n_tok, wa = q.shape
    n_pages = page_table.shape[1]
    rows = cache_k.shape[1]
    tq = 2 * SUBLANES
    assert n_pages % pages_per_step == 0 and n_tok <= tq and d_head == LANES
    steps = n_pages // pages_per_step
    q4 = q.reshape(bs, n_tok, n_heads, d_head).transpose(0, 2, 1, 3)
    q4 = jnp.pad(q4, ((0, 0), (0, 0), (0, tq - n_tok), (0, 0)))
    pad = ((0, 0), (0, rows - n_tok * n_heads), (0, 0))
    kn = jnp.pad(k_new, pad)
    vn = jnp.pad(v_new, pad)

    def page_map(i):
        return lambda b, j, pt: (layer + pt[b, n_pages - 1 - (j * pages_per_step + i)], 0, 0)

    page_specs = [pl.BlockSpec((None, rows, d_head), page_map(i)) for i in range(pages_per_step)]
    grid_spec = pltpu.PrefetchScalarGridSpec(
        num_scalar_prefetch=1,
        grid=(bs, steps),
        in_specs=[
            pl.BlockSpec((None, n_heads, tq, d_head), lambda b, j, pt: (b, 0, 0, 0)),
            pl.BlockSpec((None, rows, d_head), lambda b, j, pt: (b, 0, 0)),
            pl.BlockSpec((None, rows, d_head), lambda b, j, pt: (b, 0, 0)),
        ] + page_specs + page_specs,
        out_specs=pl.BlockSpec((None, n_heads * tq, d_head), lambda b, j, pt: (b, 0, 0)),
        scratch_shapes=[pltpu.VMEM((n_heads * tq, 1), F32)],
    )
    out = pl.pallas_call(
        functools.partial(_attn_sample_kernel, pages_per_step=pages_per_step, n_heads=n_heads,
                          tq=tq, scale=1.0 / math.sqrt(d_head)),
        grid_spec=grid_spec,
        out_shape=jax.ShapeDtypeStruct((bs, n_heads * tq, d_head), F32),
        compiler_params=_params(("parallel", "arbitrary")),
        name="attn_sample",
    )(page_table, q4, kn, vn, *([cache_k] * pages_per_step), *([cache_v] * pages_per_step))
    out = out.reshape(bs, n_heads, tq, d_head)[:, :, :n_tok]
    return out.transpose(0, 2, 1, 3).reshape(bs, n_tok, wa)


def _pool_kernel(halo_ref, u_ref, w_ref, sc_ref, o_ref, ext_ref, *, tp, pos0):
    t = pl.program_id(1)

    @pl.when(t == 0)
    def _():
        ext_ref[0:HALO, :] = halo_ref[...]

    ext_ref[HALO:HALO + tp, :] = u_ref[...]
    pos = pos0 + t * tp + lax.broadcasted_iota(jnp.int32, (tp, 1), 0)
    for g, win in enumerate(POOL_WINDOWS):
        sl = slice(g * LANES, (g + 1) * LANES)
        s = ext_ref[HALO:HALO + tp, sl]
        for i in range(1, win):
            s = s + ext_ref[HALO - i:HALO - i + tp, sl]
        cnt = jnp.minimum(pos + 1, win).astype(F32)
        diff = s / cnt - u_ref[:, sl]
        o_ref[:, sl] = _dot(diff.astype(BF16), w_ref[g].astype(BF16)) * sc_ref[:, sl]
    ext_ref[0:HALO, :] = ext_ref[tp:tp + HALO, :]


def _pool(u, halo, w, scale, *, tp, pos0):
    bsz, t_len, wp = u.shape
    assert t_len % tp == 0 and wp == len(POOL_WINDOWS) * LANES
    return pl.pallas_call(
        functools.partial(_pool_kernel, tp=tp, pos0=pos0),
        grid=(bsz, t_len // tp),
        in_specs=[
            pl.BlockSpec((None, HALO, wp), lambda b, t: (b, 0, 0)),
            pl.BlockSpec((None, tp, wp), lambda b, t: (b, t, 0)),
            pl.BlockSpec(w.shape, lambda b, t: (0, 0, 0)),
            pl.BlockSpec((1, wp), lambda b, t: (0, 0)),
        ],
        out_specs=pl.BlockSpec((None, tp, wp), lambda b, t: (b, t, 0)),
        out_shape=jax.ShapeDtypeStruct(u.shape, F32),
        scratch_shapes=[pltpu.VMEM((HALO + tp, wp), F32)],
        compiler_params=_params(("parallel", "arbitrary")),
        name="pool",
    )(halo, u, w, scale.reshape(1, wp))


def _ssm_prep_kernel(lr_ref, li_ref, ldt_ref, br_ref, bi_ref, lbr_ref, lbi_ref, bbr_ref, bbi_ref):
    lr = lr_ref[...]
    li = li_ref[...]
    dt = jnp.exp(ldt_ref[...])
    mag = jnp.exp(lr * dt)
    ang = li * dt
    lbr = mag * jnp.cos(ang)
    lbi = mag * jnp.sin(ang)
    nr = lbr - 1.0
    den = lr * lr + li * li
    fr = (nr * lr + lbi * li) / den
    fi = (lbi * lr - nr * li) / den
    br = br_ref[...]
    bi = bi_ref[...]
    lbr_ref[...] = lbr
    lbi_ref[...] = lbi
    bbr_ref[...] = fr * br - fi * bi
    bbi_ref[...] = fr * bi + fi * br


def _ssm_prep(lam_re, lam_im, log_dt, b_re, b_im):
    depth, g, p = lam_re.shape
    ch = b_re.shape[-1]
    shape = (depth, g, ch, p)
    bc = lambda a: jnp.broadcast_to(a[:, :, None, :], shape)
    spec = pl.BlockSpec((None, g, ch, p), lambda l: (l, 0, 0, 0))
    out = jax.ShapeDtypeStruct(shape, F32)
    lbr, lbi, bbr, bbi = pl.pallas_call(
        _ssm_prep_kernel,
        grid=(depth,),
        in_specs=[spec] * 5,
        out_specs=[spec] * 4,
        out_shape=[out] * 4,
        compiler_params=_params(("parallel",)),
        name="ssm_prep",
    )(bc(lam_re), bc(lam_im), jnp.broadcast_to(log_dt[:, :, None, None], shape),
      b_re.transpose(0, 1, 3, 2), b_im.transpose(0, 1, 3, 2))
    return lbr[:, :, 0, :], lbi[:, :, 0, :], bbr, bbi


def _gelu_glu(y, wglu_ref):
    g = jax.nn.gelu(y, approximate=True)
    return g * jax.nn.sigmoid(_dot(g.astype(BF16), wglu_ref[...].astype(BF16)))


def _ssm_prompt_kernel(u_ref, bh_ref, cre_ref, cim_ref, lre_ref, lim_ref, d_ref, wglu_ref,
                       y_ref, hre_ref, him_ref, buf_ref, *, lc, nb, half_in, half_out):
    c = pl.program_id(0)
    nslab = buf_ref.shape[0] // 2
    rows = 2 * nb

    @pl.when(c == 0)
    def _():
        hre_ref[...] = jnp.zeros_like(hre_ref)
        him_ref[...] = jnp.zeros_like(him_ref)

    for half in range(2):
        for b in range(nb):
            s = half * nb + b
            ub = u_ref[b, :, half * half_in:(half + 1) * half_in].astype(BF16)
            bu = _dot(ub, bh_ref[half])
            for j in range(2 * nslab):
                buf_ref[j, pl.ds(s, lc, stride=rows), :] = bu[:, j * LANES:(j + 1) * LANES]

    lam_r = [lre_ref[:, j * LANES:(j + 1) * LANES] for j in range(nslab)]
    lam_i = [lim_ref[:, j * LANES:(j + 1) * LANES] for j in range(nslab)]
    init = (tuple(hre_ref[:, j * LANES:(j + 1) * LANES] for j in range(nslab)),
            tuple(him_ref[:, j * LANES:(j + 1) * LANES] for j in range(nslab)))

    def step(k, carry):
        hr, hi = carry
        r0 = pl.multiple_of(k * rows, rows)
        new_r, new_i = [], []
        for j in range(nslab):
            xr = buf_ref[j, pl.ds(r0, rows), :]
            xi = buf_ref[nslab + j, pl.ds(r0, rows), :]
            nr = lam_r[j] * hr[j] - lam_i[j] * hi[j] + xr
            ni = lam_r[j] * hi[j] + lam_i[j] * hr[j] + xi
            buf_ref[j, pl.ds(r0, rows), :] = nr
            buf_ref[nslab + j, pl.ds(r0, rows), :] = ni
            new_r.append(nr)
            new_i.append(ni)
        return tuple(new_r), tuple(new_i)

    hr, hi = lax.fori_loop(0, lc, step, init, unroll=2)
    for j in range(nslab):
        hre_ref[:, j * LANES:(j + 1) * LANES] = hr[j]
        him_ref[:, j * LANES:(j + 1) * LANES] = hi[j]

    for b in range(nb):
        parts = []
        for half in range(2):
            s = half * nb + b
            h_re = jnp.concatenate(
                [buf_ref[j, pl.ds(s, lc, stride=rows), :] for j in range(nslab)], axis=1)
            h_im = jnp.concatenate(
                [buf_ref[nslab + j, pl.ds(s, lc, stride=rows), :] for j in range(nslab)], axis=1)
            parts.append(_dot(h_re.astype(BF16), cre_ref[half]) - _dot(h_im.astype(BF16), cim_ref[half]))
        y = jnp.concatenate(parts, axis=1) + d_ref[...] * u_ref[b]
        y_ref[b] = _gelu_glu(y, wglu_ref)


def _ssm_prompt(u, bh, cre, cim, lam_re8, lam_im8, d_skip, w_glu, *, lc):
    nb, t_len, w = u.shape
    s2 = lam_re8.shape[1]
    assert 2 * nb == SUBLANES and t_len % lc == 0 and s2 % LANES == 0
    nslab = s2 // LANES
    st = jax.ShapeDtypeStruct((2 * nb, s2), F32)
    full = lambda a: pl.BlockSpec(a.shape, lambda c: (0,) * a.ndim)
    d2 = d_skip.reshape(1, w)
    return pl.pallas_call(
        functools.partial(_ssm_prompt_kernel, lc=lc, nb=nb, half_in=w // 2, half_out=w // 2),
        grid=(t_len // lc,),
        in_specs=[pl.BlockSpec((nb, lc, w), lambda c: (0, c, 0)),
                  full(bh), full(cre), full(cim), full(lam_re8), full(lam_im8), full(d2), full(w_glu)],
        out_specs=[pl.BlockSpec((nb, lc, w), lambda c: (0, c, 0)),
                   pl.BlockSpec((2 * nb, s2), lambda c: (0, 0)),
                   pl.BlockSpec((2 * nb, s2), lambda c: (0, 0))],
        out_shape=[jax.ShapeDtypeStruct(u.shape, F32), st, st],
        scratch_shapes=[pltpu.VMEM((2 * nslab, lc * 2 * nb, LANES), F32)],
        compiler_params=_params(("arbitrary",)),
        name="ssm_prompt",
    )(u, bh, cre, cim, lam_re8, lam_im8, d2, w_glu)


def _ssm_sample_kernel(u_ref, h0r_ref, h0i_ref, bre_ref, bim_ref, cre_ref, cim_ref, lre_ref, lim_ref,
                       d_ref, wglu_ref, y_ref, hre_ref, him_ref, *, n_tok):
    hr = h0r_ref[...]
    hi = h0i_ref[...]
    lr = lre_ref[...]
    li = lim_ref[...]
    for t in range(n_tok):
        u = u_ref[t]
        ub = u.astype(BF16)
        hr, hi = (lr * hr - li * hi + _dot(ub, bre_ref[...]),
                  lr * hi + li * hr + _dot(ub, bim_ref[...]))
        y = (_dot(hr.astype(BF16), cre_ref[...]) - _dot(hi.astype(BF16), cim_ref[...])
             + d_ref[...] * u)
        y_ref[t] = _gelu_glu(y, wglu_ref)
    hre_ref[...] = hr
    him_ref[...] = hi


def _ssm_sample(u_t, h0_re, h0_im, bbd_re, bbd_im, cbd_re, cbd_im, lam_re, lam_im, d_skip, w_glu):
    n_tok, bs, w = u_t.shape
    s = h0_re.shape[1]
    st = jax.ShapeDtypeStruct((bs, s), F32)
    lam_b = lambda a: jnp.broadcast_to(a.reshape(1, s), (bs, s))
    return pl.pallas_call(
        functools.partial(_ssm_sample_kernel, n_tok=n_tok),
        out_shape=[jax.ShapeDtypeStruct(u_t.shape, F32), st, st],
        compiler_params=_params(None),
        name="ssm_sample",
    )(u_t, h0_re, h0_im, bbd_re, bbd_im, cbd_re, cbd_im, lam_b(lam_re), lam_b(lam_im),
      d_skip.reshape(1, w), w_glu)


def _block_diag_b(bb):
    depth, g, ch, p = bb.shape
    return jnp.einsum("lgcp,gh->lgchp", bb, jnp.eye(g, dtype=bb.dtype)).reshape(depth, g * ch, g * p)


def _block_diag_c(c):
    depth, g, ch, p = c.shape
    return jnp.einsum("lghp,gk->lgpkh", c, jnp.eye(g, dtype=c.dtype)).reshape(depth, g * p, g * ch)


def kernel(x_prompt, x_sample, cache_k, cache_v, state_ssm_re, state_ssm_im, state_pool, page_table,
           norm_ffn1, ffn1_w_gate, ffn1_w_up, ffn1_w_down, norm_mix, w_in, b_in, ssm_lambda_re,
           ssm_lambda_im, ssm_log_dt, ssm_b_re, ssm_b_im, ssm_c_re, ssm_c_im, ssm_d, ssm_w_glu,
           pool_w, pool_scale, norm_attn_out, norm_ssm_out, norm_pool_out, w_out, norm_ffn2,
           ffn2_w_gate, ffn2_w_up, ffn2_w_down, norm_final):
    bp, seq, d_model = x_prompt.shape
    bs, dec_seq, _ = x_sample.shape
    depth, n_pool, page, n_heads, d_head = cache_k.shape
    w_attn = n_heads * d_head
    _, n_groups, n_state = ssm_lambda_re.shape
    ssm_ch = ssm_b_re.shape[-1]
    w_ssm = n_groups * ssm_ch
    w_pool = pool_scale.shape[-1]
    pool_buf = state_pool.shape[2]
    past_len = page_table.shape[1] * page
    mp, ms = bp * seq, bs * dec_seq
    s_all = n_groups * n_state
    s_half = s_all // 2

    lb_re, lb_im, bb_re, bb_im = _ssm_prep(ssm_lambda_re, ssm_lambda_im, ssm_log_dt, ssm_b_re, ssm_b_im)
    bbd_re = _block_diag_b(bb_re).astype(BF16)
    bbd_im = _block_diag_b(bb_im).astype(BF16)
    cbd_re = _block_diag_c(ssm_c_re).astype(BF16)
    cbd_im = _block_diag_c(ssm_c_im).astype(BF16)
    lam_re = lb_re.reshape(depth, 1, s_all)
    lam_im = lb_im.reshape(depth, 1, s_all)
    hw = w_ssm // 2

    def halves_b(l):
        return jnp.stack([jnp.concatenate([bbd_re[l, h * hw:(h + 1) * hw, h * s_half:(h + 1) * s_half],
                                           bbd_im[l, h * hw:(h + 1) * hw, h * s_half:(h + 1) * s_half]],
                                          axis=1) for h in range(2)])

    def halves_c(cbd, l):
        return jnp.stack([cbd[l, h * s_half:(h + 1) * s_half, h * hw:(h + 1) * hw] for h in range(2)])

    def lam8(lam, l):
        return jnp.repeat(lam[l].reshape(2, s_half), bp, axis=0)

    cache_k2 = cache_k.reshape(depth * n_pool, page * n_heads, d_head)
    cache_v2 = cache_v.reshape(depth * n_pool, page * n_heads, d_head)
    zero_halo = jnp.zeros((bp, HALO, w_pool), F32)
    pad_t = SUBLANES - dec_seq

    xp = x_prompt.reshape(mp, d_model)
    xs = x_sample.reshape(ms, d_model)
    kp_all = jnp.zeros((depth, mp * n_heads, d_head), F32)
    vp_all = jnp.zeros((depth, mp * n_heads, d_head), F32)
    ks_all = jnp.zeros((depth, ms * n_heads, d_head), F32)
    vs_all = jnp.zeros((depth, ms * n_heads, d_head), F32)
    outs = {n: [] for n in ("hrp", "hip", "bufp", "hrs", "his", "bufs")}
    tm_p = 1024
    for l in range(depth):
        xp, xs = _ffn(xp, xs, norm_ffn1, ffn1_w_gate, ffn1_w_up, ffn1_w_down, l, tm=tm_p, tf=512)
        q, kb, vb, kp_all, vp_all, us, up = _in_proj(
            xp, norm_mix, w_in, b_in, kp_all, vp_all, l, n_heads=n_heads, w_ssm=w_ssm,
            w_pool=w_pool, tm=tm_p)
        att = _attn_prompt(q, kb, vb, bsz=bp, t_len=seq, n_heads=n_heads, d_head=d_head, blk=256,
                           hp=4)
        ssm, st_re, st_im = _ssm_prompt(us.reshape(bp, seq, w_ssm), halves_b(l), halves_c(cbd_re, l),
                                        halves_c(cbd_im, l), lam8(lam_re, l), lam8(lam_im, l),
                                        ssm_d[l], ssm_w_glu[l], lc=128)
        up3 = up.reshape(bp, seq, w_pool)
        pool = _pool(up3, zero_halo, pool_w[l], pool_scale[l], tp=512, pos0=0)
        xp = _out_proj(xp, att, ssm.reshape(mp, w_ssm), pool.reshape(mp, w_pool), norm_attn_out[l],
                       norm_ssm_out[l], norm_pool_out[l], w_out, l, tm=tm_p, tn=512)
        unhalf = lambda a: a.reshape(2, bp, n_groups // 2, n_state).transpose(1, 0, 2, 3).reshape(
            bp, n_groups, n_state)
        outs["hrp"].append(unhalf(st_re))
        outs["hip"].append(unhalf(st_im))
        outs["bufp"].append(up3[:, seq - pool_buf:, :])

        q, _, _, ks_all, vs_all, us, up = _in_proj(
            xs, norm_mix, w_in, b_in, ks_all, vs_all, l, n_heads=n_heads, w_ssm=w_ssm,
            w_pool=w_pool, tm=ms)
        att = _attn_sample(q.reshape(bs, dec_seq, w_attn),
                           ks_all[l].reshape(bs, dec_seq * n_heads, d_head),
                           vs_all[l].reshape(bs, dec_seq * n_heads, d_head), cache_k2, cache_v2,
                           page_table, l * n_pool, n_heads=n_heads, d_head=d_head, pages_per_step=8)
        u_t = us.reshape(bs, dec_seq, w_ssm).transpose(1, 0, 2)
        ssm_t, st_re, st_im = _ssm_sample(u_t, state_ssm_re[l].reshape(bs, s_all),
                                          state_ssm_im[l].reshape(bs, s_all), bbd_re[l], bbd_im[l],
                                          cbd_re[l], cbd_im[l], lam_re[l], lam_im[l], ssm_d[l],
                                          ssm_w_glu[l])
        ssm = ssm_t.transpose(1, 0, 2).reshape(ms, w_ssm)
        up3 = up.reshape(bs, dec_seq, w_pool)
        halo = jnp.pad(state_pool[l], ((0, 0), (HALO - pool_buf, 0), (0, 0)))
        pool = _pool(jnp.pad(up3, ((0, 0), (0, pad_t), (0, 0))), halo, pool_w[l], pool_scale[l],
                     tp=SUBLANES, pos0=past_len)[:, :dec_seq]
        xs = _out_proj(xs, att.reshape(ms, w_attn), ssm, pool.reshape(ms, w_pool), norm_attn_out[l],
                       norm_ssm_out[l], norm_pool_out[l], w_out, l, tm=ms, tn=512)
        xp, xs = _ffn(xp, xs, norm_ffn2, ffn2_w_gate, ffn2_w_up, ffn2_w_down, l, tm=tm_p, tf=512)
        outs["hrs"].append(st_re.reshape(bs, n_groups, n_state))
        outs["his"].append(st_im.reshape(bs, n_groups, n_state))
        outs["bufs"].append(jnp.concatenate([state_pool[l], up3], axis=1)[:, -pool_buf:])

    xp = _rmsnorm(xp, norm_final, tm=512)
    xs = _rmsnorm(xs, norm_final, tm=ms)
    st = lambda n: jnp.stack(outs[n])
    kv_p = lambda a: a.reshape(depth, bp, seq, n_heads, d_head)
    kv_s = lambda a: a.reshape(depth, bs, dec_seq, n_heads, d_head)
    return (xp.reshape(bp, seq, d_model), xs.reshape(bs, dec_seq, d_model),
            kv_p(kp_all), kv_p(vp_all), st("hrp"), st("hip"), st("bufp"),
            kv_s(ks_all), kv_s(vs_all), st("hrs"), st("his"), st("bufs"))
```

```python
import functools
import math

import jax
import jax.numpy as jnp
from jax import lax
from jax.experimental import pallas as pl
from jax.experimental.pallas import tpu as pltpu

F32 = jnp.float32
BF16 = jnp.bfloat16
RMS_EPS = 1e-6
LANES = 128
SUBLANES = 8
VMEM_LIMIT = 56 * 1024 * 1024

POOL_WINDOWS = (2, 4, 8, 16)
HALO = 16


def _params(sem, vmem=VMEM_LIMIT):
    return pltpu.CompilerParams(dimension_semantics=sem, vmem_limit_bytes=vmem)


def _rms_scale(x):
    return x * lax.rsqrt(jnp.mean(x * x, axis=-1, keepdims=True) + RMS_EPS)


def _dot(a, b):
    return jnp.dot(a, b, preferred_element_type=F32)


def _ffn_kernel(x_ref, xs_ref, g_ref, wg_ref, wu_ref, wd_ref, o_ref, os_ref, hn_ref, hns_ref):
    i = pl.program_id(0)
    f = pl.program_id(1)
    last = f == pl.num_programs(1) - 1
    tf = wg_ref.shape[1]
    fc = min(tf, 256)

    def rows(src_ref, dst_ref, h_ref):
        n_rows = src_ref.shape[0]
        rc = min(n_rows, 256)

        @pl.when(f == 0)
        def _():
            for r in range(0, n_rows, rc):
                h_ref[r:r + rc, :] = (_rms_scale(src_ref[r:r + rc, :]) * g_ref[...]).astype(BF16)
            dst_ref[...] = jnp.zeros_like(dst_ref)

        hn = h_ref[...]
        for c in range(0, tf, fc):
            gate = _dot(hn, wg_ref[:, c:c + fc].astype(BF16))
            up = _dot(hn, wu_ref[:, c:c + fc].astype(BF16))
            act = (gate * jax.nn.sigmoid(gate) * up).astype(BF16)
            dst_ref[...] += _dot(act, wd_ref[c:c + fc, :].astype(BF16))

        @pl.when(last)
        def _():
            for r in range(0, n_rows, rc):
                dst_ref[r:r + rc, :] = src_ref[r:r + rc, :] + 0.5 * dst_ref[r:r + rc, :]

    rows(x_ref, o_ref, hn_ref)
    pl.when(i == 0)(lambda: rows(xs_ref, os_ref, hns_ref))


def _ffn(x, xs, g, wg, wu, wd, layer, *, tm, tf):
    m, d = x.shape
    ms = xs.shape[0]
    depth, _, d_ff = wg.shape
    assert m % tm == 0 and d_ff % tf == 0
    nf = d_ff // tf
    once = pl.Buffered(1)
    return pl.pallas_call(
        _ffn_kernel,
        grid=(m // tm, nf),
        in_specs=[
            pl.BlockSpec((tm, d), lambda i, f: (i, 0), pipeline_mode=once),
            pl.BlockSpec((ms, d), lambda i, f: (0, 0)),
            pl.BlockSpec((None, 1, d), lambda i, f: (layer, 0, 0)),
            pl.BlockSpec((d, tf), lambda i, f: (layer, f)),
            pl.BlockSpec((d, tf), lambda i, f: (layer, f)),
            pl.BlockSpec((tf, d), lambda i, f: (layer * nf + f, 0)),
        ],
        out_specs=[pl.BlockSpec((tm, d), lambda i, f: (i, 0), pipeline_mode=once),
                   pl.BlockSpec((ms, d), lambda i, f: (0, 0))],
        out_shape=[jax.ShapeDtypeStruct((m, d), F32), jax.ShapeDtypeStruct((ms, d), F32)],
        scratch_shapes=[pltpu.VMEM((tm, d), BF16), pltpu.VMEM((ms, d), BF16)],
        compiler_params=_params(("arbitrary", "arbitrary")),
        name="ffn",
    )(x, xs, g[:, None, :], wg.reshape(depth * d, d_ff), wu.reshape(depth * d, d_ff),
      wd.reshape(depth * d_ff, d))


def _rmsnorm_kernel(x_ref, g_ref, o_ref):
    o_ref[...] = _rms_scale(x_ref[...]) * g_ref[...]


def _rmsnorm(x, g, *, tm):
    m, d = x.shape
    assert m % tm == 0
    return pl.pallas_call(
        _rmsnorm_kernel,
        grid=(m // tm,),
        in_specs=[pl.BlockSpec((tm, d), lambda i: (i, 0)), pl.BlockSpec((1, d), lambda i: (0, 0))],
        out_specs=pl.BlockSpec((tm, d), lambda i: (i, 0)),
        out_shape=jax.ShapeDtypeStruct((m, d), F32),
        compiler_params=_params(("parallel",)),
        name="final_norm",
    )(x, g.reshape(1, d))


def _in_proj_kernel(x_ref, g_ref, w_ref, b_ref, k_all_ref, v_all_ref, q_ref, kb_ref, vb_ref, k_ref,
                    v_ref, us_ref, up_ref, hn_ref, *, na, n_heads):
    del k_all_ref, v_all_ref
    n = pl.program_id(1)
    tm = x_ref.shape[0]
    rc = min(tm, 256)
    hpt = w_ref.shape[1] // LANES

    @pl.when(n == 0)
    def _():
        hn_ref[...] = (_rms_scale(x_ref[...]) * g_ref[...]).astype(BF16)

    def project(store):
        wb = w_ref[...].astype(BF16)
        for r in range(0, tm, rc):
            store(r, _dot(hn_ref[r:r + rc, :], wb) + b_ref[...])

    def to_rows(dst_ref, dtype):
        def store(r, y):
            dst_ref[r:r + rc, :] = y.astype(dtype)
        return store

    def to_heads(b16_ref, rows_ref, c):
        def store(r, y):
            b16_ref[r:r + rc, :] = y.astype(BF16)
            for j in range(hpt):
                rows_ref[pl.ds(r * n_heads + c * hpt + j, rc, stride=n_heads), :] = (
                    y[:, j * LANES:(j + 1) * LANES])
        return store

    pl.when(n < na)(functools.partial(project, to_rows(q_ref, BF16)))
    for c in range(na):
        pl.when(n == na + c)(functools.partial(project, to_heads(kb_ref, k_ref, c)))
        pl.when(n == 2 * na + c)(functools.partial(project, to_heads(vb_ref, v_ref, c)))
    pl.when(n == 3 * na)(functools.partial(project, to_rows(us_ref, F32)))
    pl.when(n == 3 * na + 1)(functools.partial(project, to_rows(up_ref, F32)))


def _in_proj(x, g, w, b, k_all, v_all, layer, *, n_heads, w_ssm, w_pool, tm):
    m, d = x.shape
    w_attn = n_heads * k_all.shape[2]
    tn = w_ssm
    assert w_pool == tn and w_attn % tn == 0 and m % tm == 0 and k_all.shape[1] == m * n_heads
    assert k_all.shape[2] == LANES
    na = w_attn // tn
    n_steps = 3 * na + 2
    assert w.shape[2] == n_steps * tn

    def attn_spec(first):
        return pl.BlockSpec((tm, tn), lambda i, n: (i, jnp.clip(n - first, 0, na - 1)))

    rows_spec = pl.BlockSpec((None, tm * n_heads, LANES), lambda i, n: (layer, i, 0))
    attn_bf16 = jax.ShapeDtypeStruct((m, w_attn), BF16)
    return pl.pallas_call(
        functools.partial(_in_proj_kernel, na=na, n_heads=n_heads),
        grid=(m // tm, n_steps),
        in_specs=[
            pl.BlockSpec((tm, d), lambda i, n: (i, 0), pipeline_mode=pl.Buffered(1)),
            pl.BlockSpec((None, 1, d), lambda i, n: (layer, 0, 0)),
            pl.BlockSpec((d, tn), lambda i, n: (layer, n)),
            pl.BlockSpec((None, 1, tn), lambda i, n: (layer, 0, n)),
            pl.BlockSpec(memory_space=pl.ANY),
            pl.BlockSpec(memory_space=pl.ANY),
        ],
        out_specs=[
            attn_spec(0), attn_spec(na), attn_spec(2 * na), rows_spec, rows_spec,
            pl.BlockSpec((tm, tn), lambda i, n: (i, 0)),
            pl.BlockSpec((tm, tn), lambda i, n: (i, 0)),
        ],
        out_shape=[
            attn_bf16, attn_bf16, attn_bf16,
            jax.ShapeDtypeStruct(k_all.shape, F32),
            jax.ShapeDtypeStruct(v_all.shape, F32),
            jax.ShapeDtypeStruct((m, w_ssm), F32),
            jax.ShapeDtypeStruct((m, w_pool), F32),
        ],
        input_output_aliases={4: 3, 5: 4},
        scratch_shapes=[pltpu.VMEM((tm, d), BF16)],
        compiler_params=_params(("parallel", "arbitrary")),
        name="in_proj",
    )(x, g[:, None, :], w.reshape(-1, w.shape[2]), b[:, None, :], k_all, v_all)


def _out_proj_kernel(x_ref, a_ref, s_ref, p_ref, ga_ref, gs_ref, gp_ref, w_ref, o_ref, m_ref,
                     *, w_attn, w_ssm):
    n = pl.program_id(1)

    @pl.when(n == 0)
    def _():
        m_ref[:, 0:w_attn] = (_rms_scale(a_ref[...]) * ga_ref[...]).astype(BF16)
        m_ref[:, w_attn:w_attn + w_ssm] = (_rms_scale(s_ref[...]) * gs_ref[...]).astype(BF16)
        m_ref[:, w_attn + w_ssm:] = (_rms_scale(p_ref[...]) * gp_ref[...]).astype(BF16)

    o_ref[...] = x_ref[...] + _dot(m_ref[...], w_ref[...].astype(BF16))


def _out_proj(x, att, ssm, pool, ga, gs, gp, w, layer, *, tm, tn):
    m, d = x.shape
    wa, ws, wp = att.shape[1], ssm.shape[1], pool.shape[1]
    mix = wa + ws + wp
    assert w.shape[1:] == (mix, d) and m % tm == 0 and d % tn == 0
    return pl.pallas_call(
        functools.partial(_out_proj_kernel, w_attn=wa, w_ssm=ws),
        grid=(m // tm, d // tn),
        in_specs=[
            pl.BlockSpec((tm, tn), lambda i, n: (i, n)),
            pl.BlockSpec((tm, wa), lambda i, n: (i, 0)),
            pl.BlockSpec((tm, ws), lambda i, n: (i, 0)),
            pl.BlockSpec((tm, wp), lambda i, n: (i, 0)),
            pl.BlockSpec((1, wa), lambda i, n: (0, 0)),
            pl.BlockSpec((1, ws), lambda i, n: (0, 0)),
            pl.BlockSpec((1, wp), lambda i, n: (0, 0)),
            pl.BlockSpec((mix, tn), lambda i, n: (layer, n)),
        ],
        out_specs=pl.BlockSpec((tm, tn), lambda i, n: (i, n)),
        out_shape=jax.ShapeDtypeStruct((m, d), F32),
        scratch_shapes=[pltpu.VMEM((tm, mix), BF16)],
        compiler_params=_params(("parallel", "arbitrary")),
        name="out_proj",
    )(x, att, ssm, pool, ga.reshape(1, wa), gs.reshape(1, ws), gp.reshape(1, wp), w.reshape(-1, d))


def _sb_terms(z):
    ls = jnp.minimum(z, 0.0) - jnp.log(1.0 + jnp.exp(-jnp.abs(z)))
    return ls, ls - z


def _split_bf16(x):
    hi = x.astype(BF16)
    lo = (x - hi.astype(F32)).astype(BF16)
    return hi, lo


def _sb_log_weights(z, after, mask):
    ls, lk = _sb_terms(z)
    if mask is not None:
        lk = jnp.where(mask, lk, 0.0)
    hi, lo = _split_bf16(lk)
    skip = _dot(hi, after) + _dot(lo, after)
    return ls + skip, skip[:, 0:1] + lk[:, 0:1]


def _sb_weights(log_w, carry, mask):
    w = jnp.exp(log_w + carry)
    if mask is not None:
        w = jnp.where(mask, w, 0.0)
    return w.astype(BF16)


def _after_matrix(n):
    row = lax.broadcasted_iota(jnp.int32, (n, n), 0)
    col = lax.broadcasted_iota(jnp.int32, (n, n), 1)
    return jnp.where(row > col, 1.0, 0.0).astype(BF16), col < row


def _attn_prompt_kernel(q_ref, k_ref, v_ref, o_ref, carry_ref, *, blk, hp, scale):
    qi = pl.program_id(2)
    after, causal = _after_matrix(blk)

    def block(start, mask):
        sls = [slice(hh * LANES, (hh + 1) * LANES) for hh in range(hp)]
        zs = [lax.dot_general(q_ref[:, sl], k_ref[pl.ds(start, blk), sl], (((1,), (1,)), ((), ())),
                              preferred_element_type=F32) * scale for sl in sls]
        lws = [_sb_log_weights(z, after, mask) for z in zs]
        for hh, (sl, (log_w, total)) in enumerate(zip(sls, lws)):
            w = _sb_weights(log_w, carry_ref[hh], mask)
            o_ref[:, sl] += _dot(w, v_ref[pl.ds(start, blk), sl])
            carry_ref[hh] += total

    o_ref[...] = jnp.zeros_like(o_ref)
    carry_ref[...] = jnp.zeros_like(carry_ref)
    block(pl.multiple_of(qi * blk, blk), causal)

    def body(i, c):
        block(pl.multiple_of((qi - 1 - i) * blk, blk), None)
        return c

    lax.fori_loop(0, qi, body, 0)


def _attn_prompt(q, k, v, *, bsz, t_len, n_heads, d_head, blk, hp):
    assert t_len % blk == 0 and d_head == LANES and n_heads % hp == 0
    nq = t_len // blk
    kv_spec = pl.BlockSpec((t_len, hp * d_head), lambda b, h, i: (b, h))
    return pl.pallas_call(
        functools.partial(_attn_prompt_kernel, blk=blk, hp=hp, scale=1.0 / math.sqrt(d_head)),
        grid=(bsz, n_heads // hp, nq),
        in_specs=[pl.BlockSpec((blk, hp * d_head), lambda b, h, i: (b * nq + i, h)), kv_spec, kv_spec],
        out_specs=pl.BlockSpec((blk, hp * d_head), lambda b, h, i: (b * nq + i, h)),
        out_shape=jax.ShapeDtypeStruct(q.shape, F32),
        scratch_shapes=[pltpu.VMEM((hp, blk, 1), F32)],
        compiler_params=_params(("parallel", "parallel", "arbitrary")),
        name="attn_prompt",
    )(q, k, v)


def _attn_sample_kernel(pt_ref, q_ref, kn_ref, vn_ref, *rest, pages_per_step, n_heads, tq, scale):
    del pt_ref
    k_refs = rest[:pages_per_step]
    v_refs = rest[pages_per_step:2 * pages_per_step]
    o_ref, carry_ref = rest[2 * pages_per_step:]
    j = pl.program_id(1)
    page = k_refs[0].shape[0] // n_heads
    after, _ = _after_matrix(page)

    def head_rows(ref, h):
        return ref[pl.ds(h, page, stride=n_heads), :].astype(BF16)

    def log_weights(k_ref, mask):
        zs = [lax.dot_general(q_ref[h], head_rows(k_ref, h), (((1,), (1,)), ((), ())),
                              preferred_element_type=F32) for h in range(n_heads)]
        z = jnp.concatenate(zs, axis=0) * scale
        return _sb_log_weights(z, after, mask)

    def walk(kv_refs, mask):
        lws = [log_weights(k_ref, mask) for k_ref, _ in kv_refs]
        carry = carry_ref[...]
        ws = []
        for log_w, total in lws:
            ws.append(_sb_weights(log_w, carry, mask))
            carry = carry + total
        carry_ref[...] = carry
        for h in range(n_heads):
            rows = slice(h * tq, (h + 1) * tq)
            acc = o_ref[rows, :]
            for w, (_, v_ref) in zip(ws, kv_refs):
                acc = acc + _dot(w[rows, :], head_rows(v_ref, h))
            o_ref[rows, :] = acc

    @pl.when(j == 0)
    def _():
        o_ref[...] = jnp.zeros_like(o_ref)
        carry_ref[...] = jnp.zeros_like(carry_ref)
        key = lax.broadcasted_iota(jnp.int32, (n_heads * tq, page), 1)
        tok = lax.broadcasted_iota(jnp.int32, (n_heads * tq, page), 0) % tq
        walk([(kn_ref, vn_ref)], key < tok)

    walk(list(zip(k_refs, v_refs)), None)


def _attn_sample(q, k_new, v_new, cache_k, cache_v, page_table, layer, *, n_heads, d_head,
                 pages_per_step):
    bs, n_tok, wa = q.shape
    n_pages = page_table.shape[1]
    rows = cache_k.shape[1]
    tq = 2 * SUBLANES
    assert n_pages % pages_per_step == 0 and n_tok <= tq and d_head == LANES
    steps = n_pages // pages_per_step
    q4 = q.reshape(bs, n_tok, n_heads, d_head).transpose(0, 2, 1, 3)
    q4 = jnp.pad(q4, ((0, 0), (0, 0), (0, tq - n_tok), (0, 0)))
    pad = ((0, 0), (0, rows - n_tok * n_heads), (0, 0))
    kn = jnp.pad(k_new, pad)
    vn = jnp.pad(v_new, pad)

    def page_map(i):
        return lambda b, j, pt: (layer + pt[b, n_pages - 1 - (j * pages_per_step + i)], 0, 0)

    page_specs = [pl.BlockSpec((None, rows, d_head), page_map(i)) for i in range(pages_per_step)]
    grid_spec = pltpu.PrefetchScalarGridSpec(
        num_scalar_prefetch=1,
        grid=(bs, steps),
        in_specs=[
            pl.BlockSpec((None, n_heads, tq, d_head), lambda b, j, pt: (b, 0, 0, 0)),
            pl.BlockSpec((None, rows, d_head), lambda b, j, pt: (b, 0, 0)),
            pl.BlockSpec((None, rows, d_head), lambda b, j, pt: (b, 0, 0)),
        ] + page_specs + page_specs,
        out_specs=pl.BlockSpec((None, n_heads * tq, d_head), lambda b, j, pt: (b, 0, 0)),
        scratch_shapes=[pltpu.VMEM((n_heads * tq, 1), F32)],
    )
    out = pl.pallas_call(
        functools.partial(_attn_sample_kernel, pages_per_step=pages_per_step, n_heads=n_heads,
                          tq=tq, scale=1.0 / math.sqrt(d_head)),
        grid_spec=grid_spec,
        out_shape=jax.ShapeDtypeStruct((bs, n_heads * tq, d_head), F32),
        compiler_params=_params(("parallel", "arbitrary")),
        name="attn_sample",
    )(page_table, q4, kn, vn, *([cache_k] * pages_per_step), *([cache_v] * pages_per_step))
    out = out.reshape(bs, n_heads, tq, d_head)[:, :, :n_tok]
    return out.transpose(0, 2, 1, 3).reshape(bs, n_tok, wa)


def _pool_kernel(halo_ref, u_ref, w_ref, sc_ref, o_ref, ext_ref, *, tp, pos0):
    t = pl.program_id(1)

    @pl.when(t == 0)
    def _():
        ext_ref[0:HALO, :] = halo_ref[...]

    ext_ref[HALO:HALO + tp, :] = u_ref[...]
    pos = pos0 + t * tp + lax.broadcasted_iota(jnp.int32, (tp, 1), 0)
    for g, win in enumerate(POOL_WINDOWS):
        sl = slice(g * LANES, (g + 1) * LANES)
        s = ext_ref[HALO:HALO + tp, sl]
        for i in range(1, win):
            s = s + ext_ref[HALO - i:HALO - i + tp, sl]
        cnt = jnp.minimum(pos + 1, win).astype(F32)
        diff = s / cnt - u_ref[:, sl]
        o_ref[:, sl] = _dot(diff.astype(BF16), w_ref[g].astype(BF16)) * sc_ref[:, sl]
    ext_ref[0:HALO, :] = ext_ref[tp:tp + HALO, :]


def _pool(u, halo, w, scale, *, tp, pos0):
    bsz, t_len, wp = u.shape
    assert t_len % tp == 0 and wp == len(POOL_WINDOWS) * LANES
    return pl.pallas_call(
        functools.partial(_pool_kernel, tp=tp, pos0=pos0),
        grid=(bsz, t_len // tp),
        in_specs=[
            pl.BlockSpec((None, HALO, wp), lambda b, t: (b, 0, 0)),
            pl.BlockSpec((None, tp, wp), lambda b, t: (b, t, 0)),
            pl.BlockSpec(w.shape, lambda b, t: (0, 0, 0)),
            pl.BlockSpec((1, wp), lambda b, t: (0, 0)),
        ],
        out_specs=pl.BlockSpec((None, tp, wp), lambda b, t: (b, t, 0)),
        out_shape=jax.ShapeDtypeStruct(u.shape, F32),
        scratch_shapes=[pltpu.VMEM((HALO + tp, wp), F32)],
        compiler_params=_params(("parallel", "arbitrary")),
        name="pool",
    )(halo, u, w, scale.reshape(1, wp))


def _ssm_prep_kernel(lr_ref, li_ref, ldt_ref, br_ref, bi_ref, lbr_ref, lbi_ref, bbr_ref, bbi_ref):
    lr = lr_ref[...]
    li = li_ref[...]
    dt = jnp.exp(ldt_ref[...])
    mag = jnp.exp(lr * dt)
    ang = li * dt
    lbr = mag * jnp.cos(ang)
    lbi = mag * jnp.sin(ang)
    nr = lbr - 1.0
    den = lr * lr + li * li
    fr = (nr * lr + lbi * li) / den
    fi = (lbi * lr - nr * li) / den
    br = br_ref[...]
    bi = bi_ref[...]
    lbr_ref[...] = lbr
    lbi_ref[...] = lbi
    bbr_ref[...] = fr * br - fi * bi
    bbi_ref[...] = fr * bi + fi * br


def _ssm_prep(lam_re, lam_im, log_dt, b_re, b_im):
    depth, g, p = lam_re.shape
    ch = b_re.shape[-1]
    shape = (depth, g, ch, p)
    bc = lambda a: jnp.broadcast_to(a[:, :, None, :], shape)
    spec = pl.BlockSpec((None, g, ch, p), lambda l: (l, 0, 0, 0))
    out = jax.ShapeDtypeStruct(shape, F32)
    lbr, lbi, bbr, bbi = pl.pallas_call(
        _ssm_prep_kernel,
        grid=(depth,),
        in_specs=[spec] * 5,
        out_specs=[spec] * 4,
        out_shape=[out] * 4,
        compiler_params=_params(("parallel",)),
        name="ssm_prep",
    )(bc(lam_re), bc(lam_im), jnp.broadcast_to(log_dt[:, :, None, None], shape),
      b_re.transpose(0, 1, 3, 2), b_im.transpose(0, 1, 3, 2))
    return lbr[:, :, 0, :], lbi[:, :, 0, :], bbr, bbi


def _gelu_glu(y, wglu_ref):
    g = jax.nn.gelu(y, approximate=True)
    return g * jax.nn.sigmoid(_dot(g.astype(BF16), wglu_ref[...].astype(BF16)))


def _ssm_prompt_kernel(u_ref, bh_ref, cre_ref, cim_ref, lre_ref, lim_ref, d_ref, wglu_ref,
                       y_ref, hre_ref, him_ref, buf_ref, *, lc, nb, half_in, half_out):
    c = pl.program_id(0)
    nslab = buf_ref.shape[0] // 2
    rows = 2 * nb

    @pl.when(c == 0)
    def _():
        hre_ref[...] = jnp.zeros_like(hre_ref)
        him_ref[...] = jnp.zeros_like(him_ref)

    for half in range(2):
        for b in range(nb):
            s = half * nb + b
            ub = u_ref[b, :, half * half_in:(half + 1) * half_in].astype(BF16)
            bu = _dot(ub, bh_ref[half])
            for j in range(2 * nslab):
                buf_ref[j, pl.ds(s, lc, stride=rows), :] = bu[:, j * LANES:(j + 1) * LANES]

    lam_r = [lre_ref[:, j * LANES:(j + 1) * LANES] for j in range(nslab)]
    lam_i = [lim_ref[:, j * LANES:(j + 1) * LANES] for j in range(nslab)]
    init = (tuple(hre_ref[:, j * LANES:(j + 1) * LANES] for j in range(nslab)),
            tuple(him_ref[:, j * LANES:(j + 1) * LANES] for j in range(nslab)))

    def step(k, carry):
        hr, hi = carry
        r0 = pl.multiple_of(k * rows, rows)
        new_r, new_i = [], []
        for j in range(nslab):
            xr = buf_ref[j, pl.ds(r0, rows), :]
            xi = buf_ref[nslab + j, pl.ds(r0, rows), :]
            nr = lam_r[j] * hr[j] - lam_i[j] * hi[j] + xr
            ni = lam_r[j] * hi[j] + lam_i[j] * hr[j] + xi
            buf_ref[j, pl.ds(r0, rows), :] = nr
            buf_ref[nslab + j, pl.ds(r0, rows), :] = ni
            new_r.append(nr)
            new_i.append(ni)
        return tuple(new_r), tuple(new_i)

    hr, hi = lax.fori_loop(0, lc, step, init, unroll=2)
    for j in range(nslab):
        hre_ref[:, j * LANES:(j + 1) * LANES] = hr[j]
        him_ref[:, j * LANES:(j + 1) * LANES] = hi[j]

    for b in range(nb):
        parts = []
        for half in range(2):
            s = half * nb + b
            h_re = jnp.concatenate(
                [buf_ref[j, pl.ds(s, lc, stride=rows), :] for j in range(nslab)], axis=1)
            h_im = jnp.concatenate(
                [buf_ref[nslab + j, pl.ds(s, lc, stride=rows), :] for j in range(nslab)], axis=1)
            parts.append(_dot(h_re.astype(BF16), cre_ref[half]) - _dot(h_im.astype(BF16), cim_ref[half]))
        y = jnp.concatenate(parts, axis=1) + d_ref[...] * u_ref[b]
        y_ref[b] = _gelu_glu(y, wglu_ref)


def _ssm_prompt(u, bh, cre, cim, lam_re8, lam_im8, d_skip, w_glu, *, lc):
    nb, t_len, w = u.shape
    s2 = lam_re8.shape[1]
    assert 2 * nb == SUBLANES and t_len % lc == 0 and s2 % LANES == 0
    nslab = s2 // LANES
    st = jax.ShapeDtypeStruct((2 * nb, s2), F32)
    full = lambda a: pl.BlockSpec(a.shape, lambda c: (0,) * a.ndim)
    d2 = d_skip.reshape(1, w)
    return pl.pallas_call(
        functools.partial(_ssm_prompt_kernel, lc=lc, nb=nb, half_in=w // 2, half_out=w // 2),
        grid=(t_len // lc,),
        in_specs=[pl.BlockSpec((nb, lc, w), lambda c: (0, c, 0)),
                  full(bh), full(cre), full(cim), full(lam_re8), full(lam_im8), full(d2), full(w_glu)],
        out_specs=[pl.BlockSpec((nb, lc, w), lambda c: (0, c, 0)),
                   pl.BlockSpec((2 * nb, s2), lambda c: (0, 0)),
                   pl.BlockSpec((2 * nb, s2), lambda c: (0, 0))],
        out_shape=[jax.ShapeDtypeStruct(u.shape, F32), st, st],
        scratch_shapes=[pltpu.VMEM((2 * nslab, lc * 2 * nb, LANES), F32)],
        compiler_params=_params(("arbitrary",)),
        name="ssm_prompt",
    )(u, bh, cre, cim, lam_re8, lam_im8, d2, w_glu)


def _ssm_sample_kernel(u_ref, h0r_ref, h0i_ref, bre_ref, bim_ref, cre_ref, cim_ref, lre_ref, lim_ref,
                       d_ref, wglu_ref, y_ref, hre_ref, him_ref, *, n_tok):
    hr = h0r_ref[...]
    hi = h0i_ref[...]
    lr = lre_ref[...]
    li = lim_ref[...]
    for t in range(n_tok):
        u = u_ref[t]
        ub = u.astype(BF16)
        hr, hi = (lr * hr - li * hi + _dot(ub, bre_ref[...]),
                  lr * hi + li * hr + _dot(ub, bim_ref[...]))
        y = (_dot(hr.astype(BF16), cre_ref[...]) - _dot(hi.astype(BF16), cim_ref[...])
             + d_ref[...] * u)
        y_ref[t] = _gelu_glu(y, wglu_ref)
    hre_ref[...] = hr
    him_ref[...] = hi


def _ssm_sample(u_t, h0_re, h0_im, bbd_re, bbd_im, cbd_re, cbd_im, lam_re, lam_im, d_skip, w_glu):
    n_tok, bs, w = u_t.shape
    s = h0_re.shape[1]
    st = jax.ShapeDtypeStruct((bs, s), F32)
    lam_b = lambda a: jnp.broadcast_to(a.reshape(1, s), (bs, s))
    return pl.pallas_call(
        functools.partial(_ssm_sample_kernel, n_tok=n_tok),
        out_shape=[jax.ShapeDtypeStruct(u_t.shape, F32), st, st],
        compiler_params=_params(None),
        name="ssm_sample",
    )(u_t, h0_re, h0_im, bbd_re, bbd_im, cbd_re, cbd_im, lam_b(lam_re), lam_b(lam_im),
      d_skip.reshape(1, w), w_glu)


def _block_diag_b(bb):
    depth, g, ch, p = bb.shape
    return jnp.einsum("lgcp,gh->lgchp", bb, jnp.eye(g, dtype=bb.dtype)).reshape(depth, g * ch, g * p)


def _block_diag_c(c):
    depth, g, ch, p = c.shape
    return jnp.einsum("lghp,gk->lgpkh", c, jnp.eye(g, dtype=c.dtype)).reshape(depth, g * p, g * ch)


def kernel(x_prompt, x_sample, cache_k, cache_v, state_ssm_re, state_ssm_im, state_pool, page_table,
           norm_ffn1, ffn1_w_gate, ffn1_w_up, ffn1_w_down, norm_mix, w_in, b_in, ssm_lambda_re,
           ssm_lambda_im, ssm_log_dt, ssm_b_re, ssm_b_im, ssm_c_re, ssm_c_im, ssm_d, ssm_w_glu,
           pool_w, pool_scale, norm_attn_out, norm_ssm_out, norm_pool_out, w_out, norm_ffn2,
           ffn2_w_gate, ffn2_w_up, ffn2_w_down, norm_final):
    bp, seq, d_model = x_prompt.shape
    bs, dec_seq, _ = x_sample.shape
    depth, n_pool, page, n_heads, d_head = cache_k.shape
    w_attn = n_heads * d_head
    _, n_groups, n_state = ssm_lambda_re.shape
    ssm_ch = ssm_b_re.shape[-1]
    w_ssm = n_groups * ssm_ch
    w_pool = pool_scale.shape[-1]
    pool_buf = state_pool.shape[2]
    past_len = page_table.shape[1] * page
    mp, ms = bp * seq, bs * dec_seq
    s_all = n_groups * n_state
    s_half = s_all // 2

    lb_re, lb_im, bb_re, bb_im = _ssm_prep(ssm_lambda_re, ssm_lambda_im, ssm_log_dt, ssm_b_re, ssm_b_im)
    bbd_re = _block_diag_b(bb_re).astype(BF16)
    bbd_im = _block_diag_b(bb_im).astype(BF16)
    cbd_re = _block_diag_c(ssm_c_re).astype(BF16)
    cbd_im = _block_diag_c(ssm_c_im).astype(BF16)
    lam_re = lb_re.reshape(depth, 1, s_all)
    lam_im = lb_im.reshape(depth, 1, s_all)
    hw = w_ssm // 2

    def halves_b(l):
        return jnp.stack([jnp.concatenate([bbd_re[l, h * hw:(h + 1) * hw, h * s_half:(h + 1) * s_half],
                                           bbd_im[l, h * hw:(h + 1) * hw, h * s_half:(h + 1) * s_half]],
                                          axis=1) for h in range(2)])

    def halves_c(cbd, l):
        return jnp.stack([cbd[l, h * s_half:(h + 1) * s_half, h * hw:(h + 1) * hw] for h in range(2)])

    def lam8(lam, l):
        return jnp.repeat(lam[l].reshape(2, s_half), bp, axis=0)

    cache_k2 = cache_k.reshape(depth * n_pool, page * n_heads, d_head)
    cache_v2 = cache_v.reshape(depth * n_pool, page * n_heads, d_head)
    zero_halo = jnp.zeros((bp, HALO, w_pool), F32)
    pad_t = SUBLANES - dec_seq

    xp = x_prompt.reshape(mp, d_model)
    xs = x_sample.reshape(ms, d_model)
    kp_all = jnp.zeros((depth, mp * n_heads, d_head), F32)
    vp_all = jnp.zeros((depth, mp * n_heads, d_head), F32)
    ks_all = jnp.zeros((depth, ms * n_heads, d_head), F32)
    vs_all = jnp.zeros((depth, ms * n_heads, d_head), F32)
    outs = {n: [] for n in ("hrp", "hip", "bufp", "hrs", "his", "bufs")}
    tm_p = 1024
    for l in range(depth):
        xp, xs = _ffn(xp, xs, norm_ffn1, ffn1_w_gate, ffn1_w_up, ffn1_w_down, l, tm=tm_p, tf=512)
        q, kb, vb, kp_all, vp_all, us, up = _in_proj(
            xp, norm_mix, w_in, b_in, kp_all, vp_all, l, n_heads=n_heads, w_ssm=w_ssm,
            w_pool=w_pool, tm=tm_p)
        att = _attn_prompt(q, kb, vb, bsz=bp, t_len=seq, n_heads=n_heads, d_head=d_head, blk=256,
                           hp=8)
        ssm, st_re, st_im = _ssm_prompt(us.reshape(bp, seq, w_ssm), halves_b(l), halves_c(cbd_re, l),
                                        halves_c(cbd_im, l), lam8(lam_re, l), lam8(lam_im, l),
                                        ssm_d[l], ssm_w_glu[l], lc=128)
        up3 = up.reshape(bp, seq, w_pool)
        pool = _pool(up3, zero_halo, pool_w[l], pool_scale[l], tp=512, pos0=0)
        xp = _out_proj(xp, att, ssm.reshape(mp, w_ssm), pool.reshape(mp, w_pool), norm_attn_out[l],
                       norm_ssm_out[l], norm_pool_out[l], w_out, l, tm=tm_p, tn=512)
        unhalf = lambda a: a.reshape(2, bp, n_groups // 2, n_state).transpose(1, 0, 2, 3).reshape(
            bp, n_groups, n_state)
        outs["hrp"].append(unhalf(st_re))
        outs["hip"].append(unhalf(st_im))
        outs["bufp"].append(up3[:, seq - pool_buf:, :])

        q, _, _, ks_all, vs_all, us, up = _in_proj(
            xs, norm_mix, w_in, b_in, ks_all, vs_all, l, n_heads=n_heads, w_ssm=w_ssm,
            w_pool=w_pool, tm=ms)
        att = _attn_sample(q.reshape(bs, dec_seq, w_attn),
                           ks_all[l].reshape(bs, dec_seq * n_heads, d_head),
                           vs_all[l].reshape(bs, dec_seq * n_heads, d_head), cache_k2, cache_v2,
                           page_table, l * n_pool, n_heads=n_heads, d_head=d_head, pages_per_step=16)
        u_t = us.reshape(bs, dec_seq, w_ssm).transpose(1, 0, 2)
        ssm_t, st_re, st_im = _ssm_sample(u_t, state_ssm_re[l].reshape(bs, s_all),
                                          state_ssm_im[l].reshape(bs, s_all), bbd_re[l], bbd_im[l],
                                          cbd_re[l], cbd_im[l], lam_re[l], lam_im[l], ssm_d[l],
                                          ssm_w_glu[l])
        ssm = ssm_t.transpose(1, 0, 2).reshape(ms, w_ssm)
        up3 = up.reshape(bs, dec_seq, w_pool)
        halo = jnp.pad(state_pool[l], ((0, 0), (HALO - pool_buf, 0), (0, 0)))
        pool = _pool(jnp.pad(up3, ((0, 0), (0, pad_t), (0, 0))), halo, pool_w[l], pool_scale[l],
                     tp=SUBLANES, pos0=past_len)[:, :dec_seq]
        xs = _out_proj(xs, att.reshape(ms, w_attn), ssm, pool.reshape(ms, w_pool), norm_attn_out[l],
                       norm_ssm_out[l], norm_pool_out[l], w_out, l, tm=ms, tn=512)
        xp, xs = _ffn(xp, xs, norm_ffn2, ffn2_w_gate, ffn2_w_up, ffn2_w_down, l, tm=tm_p, tf=512)
        outs["hrs"].append(st_re.reshape(bs, n_groups, n_state))
        outs["his"].append(st_im.reshape(bs, n_groups, n_state))
        outs["bufs"].append(jnp.concatenate([state_pool[l], up3], axis=1)[:, -pool_buf:])

    xp = _rmsnorm(xp, norm_final, tm=512)
    xs = _rmsnorm(xs, norm_final, tm=ms)
    st = lambda n: jnp.stack(outs[n])
    kv_p = lambda a: a.reshape(depth, bp, seq, n_heads, d_head)
    kv_s = lambda a: a.reshape(depth, bs, dec_seq, n_heads, d_head)
    return (xp.reshape(bp, seq, d_model), xs.reshape(bs, dec_seq, d_model),
            kv_p(kp_all), kv_p(vp_all), st("hrp"), st("hip"), st("bufp"),
            kv_s(ks_all), kv_s(vs_all), st("hrs"), st("his"), st("bufs"))
```
